```python
import math
import jax, jax.numpy as jnp
from jax import lax
import numpy as np

D_MODEL = 1024
BATCH = 16
SEQ = 2048
DEPTH = 2

PLE_DIM = 256
CONV_CH = 512
CONV_WIDTH = 31
ATTN_HEADS = 4
ATTN_HD = 64
ATTN_VD = 2 * ATTN_HD
ATTN_QK = ATTN_HEADS * 2 * ATTN_HD
ATTN_W = ATTN_HEADS * ATTN_VD
POOL_CH = 512
POOL_WINDOWS = (2, 4, 8, 16)
POOL_GROUPS = 4
POOL_GC = POOL_CH // POOL_GROUPS
N_BRANCH = 3
D_FF = 4 * D_MODEL
REL_BUCKETS = 32
REL_MAX_DIST = 128
Q_BLOCK = 128
EPS = 1e-6
IN_SPLITS = (CONV_CH, CONV_CH, ATTN_QK, ATTN_QK, ATTN_W, POOL_CH, N_BRANCH * D_MODEL)
D_IN = CONV_CH * 2 + ATTN_QK * 2 + ATTN_W + POOL_CH + N_BRANCH * D_MODEL

kernel_name = "hybrid_gated_conv_diffattn_pool_block"


def rmsnorm(x, g):
    xf = x.astype(jnp.float32)
    y = xf * lax.rsqrt(jnp.mean(xf * xf, axis=-1, keepdims=True) + EPS) * g
    return y.astype(x.dtype)


def layernorm(x, g, b):
    xf = x.astype(jnp.float32)
    mu = jnp.mean(xf, axis=-1, keepdims=True)
    xc = xf - mu
    var = jnp.mean(xc * xc, axis=-1, keepdims=True)
    return (xc * lax.rsqrt(var + EPS) * g + b).astype(x.dtype)


def split_cols(t):
    out, off = [], 0
    for w in IN_SPLITS:
        out.append(t[..., off:off + w])
        off += w
    return out


def rel_bucket(dist):
    n = jnp.maximum(dist, 0)
    max_exact = REL_BUCKETS // 2
    nf = jnp.maximum(n, 1).astype(jnp.float32)
    large = max_exact + (jnp.log(nf / max_exact) / math.log(REL_MAX_DIST / max_exact)
                         * (REL_BUCKETS - max_exact)).astype(jnp.int32)
    large = jnp.minimum(large, REL_BUCKETS - 1)
    return jnp.where(n < max_exact, n, large)


def conv_module(a, b, w_dw, b_dw, ln_g, ln_b, w_pw):
    u = a * jax.nn.sigmoid(b)
    u = lax.conv_general_dilated(u, w_dw, window_strides=(1,),
                                 padding=[(CONV_WIDTH - 1, 0)],
                                 dimension_numbers=("NWC", "WIO", "NWC"),
                                 feature_group_count=CONV_CH) + b_dw
    u = jax.nn.silu(layernorm(u, ln_g, ln_b))
    return u @ w_pw


def diff_attention(q, k, v, rel_bias, lam_p, subln_g, lambda_init):
    B, S = q.shape[0], q.shape[1]
    lp = lam_p.astype(jnp.float32)
    lam = jnp.exp(jnp.sum(lp[0] * lp[1])) - jnp.exp(jnp.sum(lp[2] * lp[3])) + lambda_init
    q = q * (ATTN_HD ** -0.5)
    outs = []
    for start in range(0, S, Q_BLOCK):
        end = start + Q_BLOCK
        qb, kb, vb = q[:, start:end], k[:, :end], v[:, :end]
        logits = jnp.einsum("bqhmd,bkhmd->bhmqk", qb, kb).astype(jnp.float32)
        dist = jnp.arange(start, end)[:, None] - jnp.arange(end)[None, :]
        bias = rel_bias[rel_bucket(dist)].astype(jnp.float32)
        bias = jnp.transpose(bias, (2, 0, 1))[None, :, None]
        logits = jnp.where(dist >= 0, logits + bias, -1e30)
        probs = jax.nn.softmax(logits, axis=-1)
        attn = probs[:, :, 0] - lam * probs[:, :, 1]
        outs.append(jnp.einsum("bhqk,bkhe->bqhe", attn.astype(vb.dtype), vb))
    o = jnp.concatenate(outs, axis=1)
    o = rmsnorm(o, subln_g) * (1.0 - lambda_init)
    return o.reshape(B, S, ATTN_W)


def pool_mixer(u, w_group, scale, w_proj):
    B, S, _ = u.shape
    uf = u.astype(jnp.float32).reshape(B, S, POOL_GROUPS, POOL_GC)
    c = jnp.pad(jnp.cumsum(uf, axis=1), ((0, 0), (1, 0), (0, 0), (0, 0)))
    t = jnp.arange(S)
    groups = []
    for g, w in enumerate(POOL_WINDOWS):
        lo = jnp.maximum(t + 1 - w, 0)
        cnt = jnp.minimum(t + 1, w).astype(jnp.float32)[None, :, None]
        mean = (c[:, 1:, g] - c[:, lo, g]) / cnt
        groups.append(mean - uf[:, :, g])
    pooled = jnp.stack(groups, axis=2)
    y = jnp.einsum("bsgc,gcd->bsgd", pooled, w_group.astype(jnp.float32))
    y = (y.reshape(B, S, POOL_CH) * scale).astype(u.dtype)
    return y @ w_proj


def setup_inputs(seed: int = 0) -> dict:
    key = jax.random.key(seed)
    ks = jax.random.split(key, 32)
    f32 = jnp.float32

    def nrm(k, shape, fan_in):
        return jax.random.normal(k, shape, f32) * (fan_in ** -0.5)

    def gain(k, shape):
        return 1.0 + 0.05 * jax.random.normal(k, shape, f32)

    def small(k, shape):
        return 0.02 * jax.random.normal(k, shape, f32)

    L = DEPTH
    return {
        "x": jax.random.normal(ks[0], (BATCH, SEQ, D_MODEL), f32),
        "p": jax.random.normal(ks[1], (DEPTH, BATCH, SEQ, PLE_DIM), f32),
        "rel_bias": 0.5 * jax.random.normal(ks[2], (REL_BUCKETS, ATTN_HEADS), f32),
        "g_pre_mix": gain(ks[3], (L, D_MODEL)),
        "w_in": nrm(ks[4], (L, D_MODEL, D_IN), D_MODEL),
        "conv_dw_w": nrm(ks[5], (L, CONV_WIDTH, 1, CONV_CH), CONV_WIDTH),
        "conv_dw_b": small(ks[6], (L, CONV_CH)),
        "conv_ln_g": gain(ks[7], (L, CONV_CH)),
        "conv_ln_b": small(ks[8], (L, CONV_CH)),
        "w_conv_out": nrm(ks[9], (L, CONV_CH, D_MODEL), CONV_CH),
        "lam_p": 0.1 * jax.random.normal(ks[10], (L, 4, ATTN_HD), f32),
        "subln_g": gain(ks[11], (L, ATTN_VD)),
        "w_attn_out": nrm(ks[12], (L, ATTN_W, D_MODEL), ATTN_W),
        "pool_w": nrm(ks[13], (L, POOL_GROUPS, POOL_GC, POOL_GC), POOL_GC),
        "pool_scale": gain(ks[14], (L, POOL_CH)),
        "w_pool_out": nrm(ks[15], (L, POOL_CH, D_MODEL), POOL_CH),
        "w_out": nrm(ks[16], (L, D_MODEL, D_MODEL), D_MODEL),
        "g_post_mix": gain(ks[17], (L, D_MODEL)),
        "g_pre_mlp": gain(ks[18], (L, D_MODEL)),
        "w_mlp_in": nrm(ks[19], (L, D_MODEL, D_FF), D_MODEL),
        "w_mlp_out": nrm(ks[20], (L, D_FF, D_MODEL), D_FF),
        "g_post_mlp": gain(ks[21], (L, D_MODEL)),
        "w_ple_proj": nrm(ks[22], (L, PLE_DIM, D_MODEL), PLE_DIM),
        "w_ple_gate": nrm(ks[23], (L, D_MODEL, D_MODEL), D_MODEL),
    }


def reference(x, p, rel_bias, g_pre_mix, w_in, conv_dw_w, conv_dw_b, conv_ln_g, conv_ln_b,
              w_conv_out, lam_p, subln_g, w_attn_out, pool_w, pool_scale, w_pool_out,
              w_out, g_post_mix, g_pre_mlp, w_mlp_in, w_mlp_out, g_post_mlp,
              w_ple_proj, w_ple_gate):
    B, S, D = x.shape
    for i in range(DEPTH):
        lambda_init = 0.8 - 0.6 * math.exp(-0.3 * i)
        h = rmsnorm(x, g_pre_mix[i])
        ca, cb, q, k, v, pu, gl = split_cols(h @ w_in[i])
        y_conv = conv_module(ca, cb, conv_dw_w[i], conv_dw_b[i], conv_ln_g[i], conv_ln_b[i],
                             w_conv_out[i])
        o = diff_attention(q.reshape(B, S, ATTN_HEADS, 2, ATTN_HD),
                           k.reshape(B, S, ATTN_HEADS, 2, ATTN_HD),
                           v.reshape(B, S, ATTN_HEADS, ATTN_VD),
                           rel_bias, lam_p[i], subln_g[i], lambda_init)
        y_attn = o @ w_attn_out[i]
        y_pool = pool_mixer(pu, pool_w[i], pool_scale[i], w_pool_out[i])
        gates = jax.nn.sigmoid(gl).reshape(B, S, N_BRANCH, D)
        merged = gates[:, :, 0] * y_conv + gates[:, :, 1] * y_attn + gates[:, :, 2] * y_pool
        x = x + rmsnorm(merged @ w_out[i], g_post_mix[i])
        h = rmsnorm(x, g_pre_mlp[i])
        f = jnp.square(jax.nn.relu(h @ w_mlp_in[i])) @ w_mlp_out[i]
        x = x + rmsnorm(f, g_post_mlp[i])
        x = x + jax.nn.sigmoid(x @ w_ple_gate[i]) * (p[i] @ w_ple_proj[i])
    return x
```

```python
import functools
import math

import jax
import jax.numpy as jnp
import numpy as np
from jax import lax
from jax.experimental import pallas as pl
from jax.experimental.pallas import tpu as pltpu

F32 = jnp.float32
BF16 = jnp.bfloat16

D_MODEL = 1024
PLE_DIM = 256
CONV_CH = 512
CONV_WIDTH = 31
ATTN_HEADS = 4
ATTN_HD = 64
ATTN_VD = 2 * ATTN_HD
ATTN_W = ATTN_HEADS * ATTN_VD
POOL_CH = 512
POOL_WINDOWS = (2, 4, 8, 16)
POOL_GC = POOL_CH // len(POOL_WINDOWS)
N_BRANCH = 3
D_FF = 4 * D_MODEL
REL_BUCKETS = 32
REL_MAX_DIST = 128
EPS = 1e-6
MASK_VALUE = -1e30

TM_PROJ = 512
TM_MIX = 512
TM_MLP = 512
TQ = 256
TK = 256
HALO = 32
CONV_RB = 64
SUBLANES = 8
VMEM_LIMIT = 56 * 1024 * 1024


def _sigmoid(x):
    return 1.0 / (1.0 + jnp.exp(-x))


def _rmsnorm(x, g):
    return x * lax.rsqrt(jnp.mean(x * x, axis=-1, keepdims=True) + EPS) * g


def _resident(shape):
    zeros = (0,) * len(shape)
    return pl.BlockSpec(shape, lambda *_: zeros, pipeline_mode=pl.Buffered(1))


def _proj_in_kernel(x_ref, g_ref, w_ref, u_ref, q_ref, k_ref, v_ref, pu_ref, gl_ref):
    h = _rmsnorm(x_ref[...], g_ref[...]).astype(BF16)

    def proj(c0, width):
        return jnp.dot(h, w_ref[:, c0:c0 + width], preferred_element_type=F32)

    ca = proj(0, CONV_CH)
    cb = proj(CONV_CH, CONV_CH)
    u_ref[...] = (ca * _sigmoid(cb)).astype(BF16)
    off = 2 * CONV_CH
    q_ref[...] = (proj(off, ATTN_W) * (ATTN_HD ** -0.5)).astype(BF16)
    k_ref[...] = proj(off + ATTN_W, ATTN_W).astype(BF16)
    v_ref[...] = proj(off + 2 * ATTN_W, ATTN_W).astype(BF16)
    pu_ref[...] = proj(off + 3 * ATTN_W, POOL_CH).astype(BF16)
    off = off + 3 * ATTN_W + POOL_CH
    for c in range(0, N_BRANCH * D_MODEL, 512):
        gl_ref[:, c:c + 512] = proj(off + c, 512).astype(BF16)


def _proj_in(x2, g, w_bf):
    n = x2.shape[0]
    d_in = w_bf.shape[1]
    row = lambda width: pl.BlockSpec((TM_PROJ, width), lambda i: (i, 0))
    outs = (CONV_CH, ATTN_W, ATTN_W, ATTN_W, POOL_CH, N_BRANCH * D_MODEL)
    return pl.pallas_call(
        _proj_in_kernel,
        grid=(n // TM_PROJ,),
        in_specs=[row(D_MODEL), _resident((1, D_MODEL)), _resident((D_MODEL, d_in))],
        out_specs=[row(w) for w in outs],
        out_shape=[jax.ShapeDtypeStruct((n, w), BF16) for w in outs],
        compiler_params=pltpu.CompilerParams(
            dimension_semantics=("arbitrary",), vmem_limit_bytes=VMEM_LIMIT),
        name="proj_in",
    )(x2, g, w_bf)


def _rel_bucket_table(n_max):
    n = np.arange(n_max, dtype=np.int32)
    max_exact = REL_BUCKETS // 2
    nf = np.maximum(n, 1).astype(np.float32)
    large = max_exact + (np.log(nf / max_exact) / np.float32(math.log(REL_MAX_DIST / max_exact))
                         * (REL_BUCKETS - max_exact)).astype(np.int32)
    large = np.minimum(large, REL_BUCKETS - 1)
    return np.where(n < max_exact, n, large)


def _near_bias_tiles(rel_bias):
    r = np.arange(TQ)[:, None]
    c = np.arange(2 * TK)[None, :]
    dist = np.stack([r - c, TK + r - c])
    table = _rel_bucket_table(TQ + TK)
    assert (_rel_bucket_table(8 * TK)[TK + 1:] == REL_BUCKETS - 1).all()
    bucket = table[np.maximum(dist, 0)]
    shifted = rel_bias.astype(F32) - rel_bias[REL_BUCKETS - 1].astype(F32)
    tiles = jnp.transpose(shifted[bucket], (3, 0, 1, 2))
    return jnp.where(dist[None] >= 0, tiles, MASK_VALUE)


def _attn_kernel(q_ref, k_ref, v_ref, bias_ref, lamp_ref, g_ref, o_ref, *, lambda_init):
    i = pl.program_id(2)
    q = q_ref[...]
    lane = lax.broadcasted_iota(jnp.int32, q.shape, 1)
    zero = jnp.zeros_like(q)
    qs = (jnp.where(lane < ATTN_HD, q, zero), jnp.where(lane >= ATTN_HD, q, zero))

    def logits(qm, kk):
        return lax.dot_general(qm, kk, (((1,), (1,)), ((), ())), preferred_element_type=F32)

    ws = pl.multiple_of(jnp.maximum(i - 1, 0) * TK, TK)
    kn = k_ref[pl.ds(ws, 2 * TK), :]
    vn = v_ref[pl.ds(ws, 2 * TK), :]
    bias = bias_ref[...]
    carry = []
    for qm in qs:
        s = logits(qm, kn) + bias
        m = jnp.max(s, axis=-1, keepdims=True)
        p = jnp.exp(s - m)
        l = jnp.sum(p, axis=-1, keepdims=True)
        acc = jnp.dot(p.astype(BF16), vn, preferred_element_type=F32)
        carry += [m, l, acc]

    def body(j, carry):
        ks = pl.multiple_of(j * TK, TK)
        kj = k_ref[pl.ds(ks, TK), :]
        vj = v_ref[pl.ds(ks, TK), :]
        out = []
        for mi, qm in enumerate(qs):
            m, l, acc = carry[3 * mi:3 * mi + 3]
            s = logits(qm, kj)
            m_new = jnp.maximum(m, jnp.max(s, axis=-1, keepdims=True))
            alpha = jnp.exp(m - m_new)
            p = jnp.exp(s - m_new)
            l = alpha * l + jnp.sum(p, axis=-1, keepdims=True)
            acc = alpha * acc + jnp.dot(p.astype(BF16), vj, preferred_element_type=F32)
            out += [m_new, l, acc]
        return tuple(out)

    _, l0, a0, _, l1, a1 = lax.fori_loop(0, jnp.maximum(i - 1, 0), body, tuple(carry))

    lp = lamp_ref[...]
    lam = (jnp.exp(jnp.sum(lp[0:1] * lp[1:2], axis=-1, keepdims=True))
           - jnp.exp(jnp.sum(lp[2:3] * lp[3:4], axis=-1, keepdims=True)) + lambda_init)
    o = a0 / l0 - lam * (a1 / l1)
    o_ref[...] = (_rmsnorm(o, g_ref[...]) * (1.0 - lambda_init)).astype(BF16)


def _attention(q, k, v, bias_tiles, lam_p, subln_g, batch, seq, lambda_init):
    n = batch * seq
    nq = seq // TQ
    kv_spec = pl.BlockSpec((seq, ATTN_VD), lambda b, h, i: (b, h))
    return pl.pallas_call(
        functools.partial(_attn_kernel, lambda_init=lambda_init),
        grid=(batch, ATTN_HEADS, nq),
        in_specs=[
            pl.BlockSpec((TQ, ATTN_VD), lambda b, h, i: (b * nq + i, h)),
            kv_spec, kv_spec,
            pl.BlockSpec((None, None, TQ, 2 * TK), lambda b, h, i: (h, jnp.minimum(i, 1), 0, 0)),
            pl.BlockSpec((4, ATTN_HD), lambda b, h, i: (0, 0)),
            pl.BlockSpec((1, ATTN_VD), lambda b, h, i: (0, 0)),
        ],
        out_specs=pl.BlockSpec((TQ, ATTN_VD), lambda b, h, i: (b * nq + i, h)),
        out_shape=jax.ShapeDtypeStruct((n, ATTN_W), BF16),
        compiler_params=pltpu.CompilerParams(
            dimension_semantics=("arbitrary", "arbitrary", "arbitrary"),
            vmem_limit_bytes=VMEM_LIMIT),
        name="diff_attn",
    )(q, k, v, bias_tiles, lam_p, subln_g)


def _mixer_kernel(x_ref, u_ref, uh_ref, pu_ref, ph_ref, o_ref, gl_ref,
                  dww_ref, dwb_ref, lng_ref, lnb_ref, wco_ref, wao_ref,
                  pw_ref, ps_ref, wpo_ref, wout_ref, gpost_ref,
                  out_ref,
                  sh_ref, pext_ref, cs_ref, pool_ref, merged_ref, *, tiles_per_seq):
    tm = TM_MIX
    t_in_seq = lax.rem(pl.program_id(0), tiles_per_seq)
    keep = t_in_seq > 0

    sh_ref[0, 0:HALO, :] = jnp.where(keep, uh_ref[...].astype(F32), 0.0)
    sh_ref[0, HALO:HALO + tm, :] = u_ref[...].astype(F32)
    span = tm + HALO - SUBLANES
    for r in range(1, SUBLANES):
        sh_ref[r, 0:span, :] = sh_ref[0, r:r + span, :]

    first_off = HALO - (CONV_WIDTH - 1)

    def conv_block(rb, _):
        base = pl.multiple_of(rb * CONV_RB, CONV_RB)
        acc = jnp.zeros((CONV_RB, CONV_CH), F32)
        for j in range(CONV_WIDTH):
            off = first_off + j
            rows = pl.ds(base + (off // SUBLANES) * SUBLANES, CONV_RB)
            acc = acc + sh_ref[off % SUBLANES, rows, :] * dww_ref[j:j + 1, :]
        acc = acc + dwb_ref[...]
        mu = jnp.mean(acc, axis=-1, keepdims=True)
        xc = acc - mu
        var = jnp.mean(xc * xc, axis=-1, keepdims=True)
        y = xc * lax.rsqrt(var + EPS) * lng_ref[...] + lnb_ref[...]
        cs_ref[pl.ds(base, CONV_RB), :] = (y * _sigmoid(y)).astype(BF16)
        return 0

    lax.fori_loop(0, tm // CONV_RB, conv_block, 0)

    pext_ref[0:HALO, :] = jnp.where(keep, ph_ref[...].astype(F32), 0.0)
    pext_ref[HALO:HALO + tm, :] = pu_ref[...].astype(F32)
    pos = t_in_seq * tm + lax.broadcasted_iota(jnp.int32, (tm, POOL_GC), 0)
    for g, w in enumerate(POOL_WINDOWS):
        cols = slice(g * POOL_GC, (g + 1) * POOL_GC)
        e = pext_ref[HALO:HALO + tm, cols]
        tot = e
        for j in range(1, w):
            tot = tot + pext_ref[HALO - j:HALO - j + tm, cols]
        cnt = jnp.minimum(pos + 1, w).astype(F32)
        pooled = tot / cnt - e
        yg = jnp.dot(pooled.astype(BF16), pw_ref[g], preferred_element_type=F32)
        pool_ref[:, cols] = (yg * ps_ref[:, cols]).astype(BF16)

    nc = 256
    for c in range(0, D_MODEL, nc):
        cols = slice(c, c + nc)
        y_conv = jnp.dot(cs_ref[...], wco_ref[:, cols], preferred_element_type=F32)
        y_attn = jnp.dot(o_ref[...], wao_ref[:, cols], preferred_element_type=F32)
        y_pool = jnp.dot(pool_ref[...], wpo_ref[:, cols], preferred_element_type=F32)
        g0 = _sigmoid(gl_ref[:, c:c + nc].astype(F32))
        g1 = _sigmoid(gl_ref[:, D_MODEL + c:D_MODEL + c + nc].astype(F32))
        g2 = _sigmoid(gl_ref[:, 2 * D_MODEL + c:2 * D_MODEL + c + nc].astype(F32))
        merged_ref[:, cols] = (g0 * y_conv + g1 * y_attn + g2 * y_pool).astype(BF16)

    mix = jnp.dot(merged_ref[...], wout_ref[...], preferred_element_type=F32)
    out_ref[...] = x_ref[...] + _rmsnorm(mix, gpost_ref[...])


def _mixer(x2, u, pu, o, gl, dww, dwb, lng, lnb, wco, wao, pw, ps, wpo, wout, gpost, seq):
    n = x2.shape[0]
    tm = TM_MIX
    tiles_per_seq = seq // tm
    row = lambda width: pl.BlockSpec((tm, width), lambda i: (i, 0))
    halo = lambda width: pl.BlockSpec(
        (HALO, width), lambda i: (jnp.maximum(i * (tm // HALO) - 1, 0), 0))
    return pl.pallas_call(
        functools.partial(_mixer_kernel, tiles_per_seq=tiles_per_seq),
        grid=(n // tm,),
        in_specs=[
            row(D_MODEL), row(CONV_CH), halo(CONV_CH), row(POOL_CH), halo(POOL_CH),
            row(ATTN_W), row(N_BRANCH * D_MODEL),
            _resident((CONV_WIDTH, CONV_CH)), _resident((1, CONV_CH)),
            _resident((1, CONV_CH)), _resident((1, CONV_CH)),
            _resident((CONV_CH, D_MODEL)), _resident((ATTN_W, D_MODEL)),
            _resident((len(POOL_WINDOWS), POOL_GC, POOL_GC)), _resident((1, POOL_CH)),
            _resident((POOL_CH, D_MODEL)), _resident((D_MODEL, D_MODEL)),
            _resident((1, D_MODEL)),
        ],
        out_specs=row(D_MODEL),
        out_shape=jax.ShapeDtypeStruct((n, D_MODEL), F32),
        scratch_shapes=[
            pltpu.VMEM((SUBLANES, tm + HALO, CONV_CH), F32),
            pltpu.VMEM((tm + HALO, POOL_CH), F32),
            pltpu.VMEM((tm, CONV_CH), BF16),
            pltpu.VMEM((tm, POOL_CH), BF16),
            pltpu.VMEM((tm, D_MODEL), BF16),
        ],
        compiler_params=pltpu.CompilerParams(
            dimension_semantics=("arbitrary",), vmem_limit_bytes=VMEM_LIMIT),
        name="mixer",
    )(x2, u, u, pu, pu, o, gl, dww, dwb, lng, lnb, wco, wao, pw, ps, wpo, wout, gpost)


def _mlp_kernel(x_ref, p_ref, gpre_ref, w1_ref, w2_ref, gpost_ref, wpp_ref, wpg_ref,
                out_ref, hid_ref):
    x = x_ref[...]
    h = _rmsnorm(x, gpre_ref[...]).astype(BF16)
    fc = 512
    for c in range(0, D_FF, fc):
        a = jnp.dot(h, w1_ref[:, c:c + fc], preferred_element_type=F32)
        a = jnp.maximum(a, 0.0)
        hid_ref[:, c:c + fc] = (a * a).astype(BF16)
    f = jnp.dot(hid_ref[...], w2_ref[...], preferred_element_type=F32)
    x = x + _rmsnorm(f, gpost_ref[...])
    gate = _sigmoid(jnp.dot(x.astype(BF16), wpg_ref[...], preferred_element_type=F32))
    pe = jnp.dot(p_ref[...].astype(BF16), wpp_ref[...], preferred_element_type=F32)
    out_ref[...] = x + gate * pe


def _mlp(x2, p2, gpre, w1, w2, gpost, wpp, wpg):
    n = x2.shape[0]
    tm = TM_MLP
    row = lambda width: pl.BlockSpec((tm, width), lambda i: (i, 0))
    return pl.pallas_call(
        _mlp_kernel,
        grid=(n // tm,),
        in_specs=[
            row(D_MODEL), row(PLE_DIM), _resident((1, D_MODEL)),
            _resident((D_MODEL, D_FF)), _resident((D_FF, D_MODEL)), _resident((1, D_MODEL)),
            _resident((PLE_DIM, D_MODEL)), _resident((D_MODEL, D_MODEL)),
        ],
        out_specs=row(D_MODEL),
        out_shape=jax.ShapeDtypeStruct((n, D_MODEL), F32),
        scratch_shapes=[pltpu.VMEM((tm, D_FF), BF16)],
        compiler_params=pltpu.CompilerParams(
            dimension_semantics=("arbitrary",), vmem_limit_bytes=VMEM_LIMIT),
        name="mlp_ple",
    )(x2, p2, gpre, w1, w2, gpost, wpp, wpg)


def kernel(x, p, rel_bias, g_pre_mix, w_in, conv_dw_w, conv_dw_b, conv_ln_g, conv_ln_b, w_conv_out, lam_p, subln_g, w_attn_out, pool_w, pool_scale, w_pool_out, w_out, g_post_mix, g_pre_mlp, w_mlp_in, w_mlp_out, g_post_mlp, w_ple_proj, w_ple_gate):
    batch, seq, d = x.shape
    depth = w_in.shape[0]
    assert d == D_MODEL and seq % TM_MIX == 0 and seq % TQ == 0 and seq >= 2 * TK
    n = batch * seq
    x2 = x.reshape(n, d)
    bias_tiles = _near_bias_tiles(rel_bias)
    row = lambda a: a.reshape(1, -1)
    for i in range(depth):
        lambda_init = 0.8 - 0.6 * math.exp(-0.3 * i)
        u, q, k, v, pu, gl = _proj_in(x2, row(g_pre_mix[i]), w_in[i].astype(BF16))
        o = _attention(q, k, v, bias_tiles, lam_p[i], row(subln_g[i]), batch, seq, lambda_init)
        x2 = _mixer(x2, u, pu, o, gl,
                    conv_dw_w[i].reshape(CONV_WIDTH, CONV_CH), row(conv_dw_b[i]),
                    row(conv_ln_g[i]), row(conv_ln_b[i]),
                    w_conv_out[i].astype(BF16), w_attn_out[i].astype(BF16),
                    pool_w[i].astype(BF16), row(pool_scale[i]),
                    w_pool_out[i].astype(BF16), w_out[i].astype(BF16),
                    row(g_post_mix[i]), seq)
        x2 = _mlp(x2, p[i].reshape(n, PLE_DIM), row(g_pre_mlp[i]),
                  w_mlp_in[i].astype(BF16), w_mlp_out[i].astype(BF16), row(g_post_mlp[i]),
                  w_ple_proj[i].astype(BF16), w_ple_gate[i].astype(BF16))
    return x2.reshape(batch, seq, d)
```

```python
import functools
import math

import jax
import jax.numpy as jnp
import numpy as np
from jax import lax
from jax.experimental import pallas as pl
from jax.experimental.pallas import tpu as pltpu

F32 = jnp.float32
BF16 = jnp.bfloat16

D_MODEL = 1024
PLE_DIM = 256
CONV_CH = 512
CONV_WIDTH = 31
ATTN_HEADS = 4
ATTN_HD = 64
ATTN_VD = 2 * ATTN_HD
ATTN_W = ATTN_HEADS * ATTN_VD
POOL_CH = 512
POOL_WINDOWS = (2, 4, 8, 16)
POOL_GC = POOL_CH // len(POOL_WINDOWS)
N_BRANCH = 3
D_FF = 4 * D_MODEL
REL_BUCKETS = 32
REL_MAX_DIST = 128
EPS = 1e-6
MASK_VALUE = -1e30

TM_PROJ = 512
TM_MIX = 512
TM_MLP = 512
TQ = 512
TK = 512
SUB = 128
ONES_ROWS = 16
ACC_ROWS = ATTN_VD + ONES_ROWS
LOG2E = math.log2(math.e)
HALO = 32
CONV_RB = 64
SUBLANES = 8
VMEM_LIMIT = 56 * 1024 * 1024


def _sigmoid(x):
    return 1.0 / (1.0 + jnp.exp(-x))


def _rmsnorm(x, g):
    return x * lax.rsqrt(jnp.mean(x * x, axis=-1, keepdims=True) + EPS) * g


def _resident(shape):
    zeros = (0,) * len(shape)
    return pl.BlockSpec(shape, lambda *_: zeros, pipeline_mode=pl.Buffered(1))


V_COL = 2 * CONV_CH + 2 * ATTN_W


def _proj_in_kernel(x_ref, g_ref, w_ref, wvt_ref, u_ref, q_ref, k_ref, vt_ref, pu_ref, gl_ref):
    h = _rmsnorm(x_ref[...], g_ref[...]).astype(BF16)

    def proj(c0, width):
        return jnp.dot(h, w_ref[:, c0:c0 + width], preferred_element_type=F32)

    ca = proj(0, CONV_CH)
    cb = proj(CONV_CH, CONV_CH)
    u_ref[...] = (ca * _sigmoid(cb)).astype(BF16)
    off = 2 * CONV_CH
    q_ref[...] = (proj(off, ATTN_W) * (ATTN_HD ** -0.5 * LOG2E)).astype(BF16)
    k_ref[...] = proj(off + ATTN_W, ATTN_W).astype(BF16)
    vt_ref[...] = lax.dot_general(wvt_ref[...], h, (((1,), (1,)), ((), ())),
                                  preferred_element_type=F32).astype(BF16)
    pu_ref[...] = proj(V_COL + ATTN_W, POOL_CH).astype(BF16)
    off = V_COL + ATTN_W + POOL_CH
    for c in range(0, N_BRANCH * D_MODEL, 512):
        gl_ref[:, c:c + 512] = proj(off + c, 512).astype(BF16)


def _proj_in(x2, g, w_bf, wvt_bf):
    n = x2.shape[0]
    d_in = w_bf.shape[1]
    row = lambda width: pl.BlockSpec((TM_PROJ, width), lambda i: (i, 0))
    widths = (CONV_CH, ATTN_W, ATTN_W, None, POOL_CH, N_BRANCH * D_MODEL)
    out_specs = [row(w) if w else pl.BlockSpec((ATTN_W, TM_PROJ), lambda i: (0, i)) for w in widths]
    out_shape = [jax.ShapeDtypeStruct((n, w) if w else (ATTN_W, n), BF16) for w in widths]
    return pl.pallas_call(
        _proj_in_kernel,
        grid=(n // TM_PROJ,),
        in_specs=[row(D_MODEL), _resident((1, D_MODEL)), _resident((D_MODEL, d_in)),
                  _resident((ATTN_W, D_MODEL))],
        out_specs=out_specs,
        out_shape=out_shape,
        compiler_params=pltpu.CompilerParams(
            dimension_semantics=("arbitrary",), vmem_limit_bytes=VMEM_LIMIT),
        name="proj_in",
    )(x2, g, w_bf, wvt_bf)


def _rel_bucket_table(n_max):
    n = np.arange(n_max, dtype=np.int32)
    max_exact = REL_BUCKETS // 2
    nf = np.maximum(n, 1).astype(np.float32)
    large = max_exact + (np.log(nf / max_exact) / np.float32(math.log(REL_MAX_DIST / max_exact))
                         * (REL_BUCKETS - max_exact)).astype(np.int32)
    large = np.minimum(large, REL_BUCKETS - 1)
    return np.where(n < max_exact, n, large)


def _near_bias_tiles(rel_bias):
    assert (_rel_bucket_table(8 * TK)[TK + 1:] == REL_BUCKETS - 1).all()
    heads = rel_bias.shape[1]
    dist = np.arange(-(TK - 1), TK + TQ)
    bucket = _rel_bucket_table(TK + TQ)[np.maximum(dist, 0)]
    shifted = (rel_bias.astype(F32) - rel_bias[REL_BUCKETS - 1].astype(F32)) * LOG2E
    line = jnp.where(dist[None, :] >= 0, shifted[bucket].T, MASK_VALUE)
    span = TK + TQ - 1
    tiles = []
    for d in range(2):
        a = lax.slice_in_dim(line, d * TK, d * TK + span, axis=1)
        a = jnp.pad(a, ((0, 0), (0, 1)))
        m = jnp.tile(a, (1, TK))[:, :TK * span].reshape(heads, TK, span)
        tiles.append(m[:, :, TK - 1:TK - 1 + TQ])
    return jnp.stack(tiles, axis=1)


def _attn_kernel(q_ref, k_ref, vt_ref, bias_ref, lamp_ref, g_ref, o_ref,
                 qb_ref, m_ref, acc_ref, s_ref, p_ref, *, lambda_init):
    i = pl.program_id(2)
    nsub = TQ // SUB
    lane = lax.broadcasted_iota(jnp.int32, (SUB, ATTN_VD), 1)
    for c in range(nsub):
        qc = q_ref[c * SUB:(c + 1) * SUB, :]
        zero = jnp.zeros_like(qc)
        qb_ref[c, 0:SUB, :] = jnp.where(lane < ATTN_HD, qc, zero)
        qb_ref[c, SUB:2 * SUB, :] = jnp.where(lane >= ATTN_HD, qc, zero)
        m_ref[c] = jnp.full((1, 2 * SUB), MASK_VALUE, F32)
        acc_ref[c] = jnp.zeros((ACC_ROWS, 2 * SUB), F32)

    def step(ks, n_keys, variant):
        tile_max = []
        for c in range(nsub):
            nk = n_keys[c]
            s = lax.dot_general(k_ref[pl.ds(ks, nk), :], qb_ref[c], (((1,), (1,)), ((), ())),
                                preferred_element_type=F32)
            if variant is not None:
                b = bias_ref[variant, 0:nk, c * SUB:(c + 1) * SUB]
                s = s + jnp.concatenate([b, b], axis=1)
            s_ref[c, 0:nk, :] = s
            tile_max.append(jnp.max(s, axis=0, keepdims=True))
        alphas = []
        for c in range(nsub):
            nk = n_keys[c]
            m_old = m_ref[c]
            m_new = jnp.maximum(m_old, tile_max[c])
            alpha = jnp.exp2(m_old - m_new)
            p_ref[c, 0:nk, :] = jnp.exp2(s_ref[c, 0:nk, :] - m_new).astype(BF16)
            m_ref[c] = m_new
            alphas.append(alpha)
        for c in range(nsub):
            nk = n_keys[c]
            vte = jnp.concatenate([vt_ref[:, pl.ds(ks, nk)], jnp.ones((ONES_ROWS, nk), BF16)], axis=0)
            pv = jnp.dot(vte, p_ref[c, 0:nk, :], preferred_element_type=F32)
            acc_ref[c] = alphas[c] * acc_ref[c] + pv

    full = (TK,) * nsub

    def far(j, carry):
        step(pl.multiple_of(j * TK, TK), full, None)
        return carry

    lax.fori_loop(0, jnp.maximum(i - 1, 0), far, 0)

    @pl.when(i >= 1)
    def _():
        step(pl.multiple_of((i - 1) * TK, TK), full, 1)

    step(pl.multiple_of(i * TK, TK), tuple((c + 1) * SUB for c in range(nsub)), 0)

    lp = lamp_ref[...]
    lam = (jnp.exp(jnp.sum(lp[0:1] * lp[1:2], axis=-1, keepdims=True))
           - jnp.exp(jnp.sum(lp[2:3] * lp[3:4], axis=-1, keepdims=True)) + lambda_init)
    for c in range(nsub):
        acc = acc_ref[c, 0:ATTN_VD, :]
        l = acc_ref[c, ATTN_VD:ATTN_VD + 1, :]
        ot = acc[:, :SUB] / l[:, :SUB] - lam * (acc[:, SUB:] / l[:, SUB:])
        o = _rmsnorm(ot.T, g_ref[...]) * (1.0 - lambda_init)
        o_ref[c * SUB:(c + 1) * SUB, :] = o.astype(BF16)


def _attention(q, k, vt, bias_tiles, lam_p, subln_g, batch, seq, lambda_init):
    n = batch * seq
    nq = seq // TQ
    nsub = TQ // SUB
    return pl.pallas_call(
        functools.partial(_attn_kernel, lambda_init=lambda_init),
        grid=(batch, ATTN_HEADS, nq),
        in_specs=[
            pl.BlockSpec((TQ, ATTN_VD), lambda b, h, i: (b * nq + i, h)),
            pl.BlockSpec((seq, ATTN_VD), lambda b, h, i: (b, h)),
            pl.BlockSpec((ATTN_VD, seq), lambda b, h, i: (h, b)),
            pl.BlockSpec((None, 2, TK, TQ), lambda b, h, i: (h, 0, 0, 0)),
            pl.BlockSpec((4, ATTN_HD), lambda b, h, i: (0, 0)),
            pl.BlockSpec((1, ATTN_VD), lambda b, h, i: (0, 0)),
        ],
        out_specs=pl.BlockSpec((TQ, ATTN_VD), lambda b, h, i: (b * nq + i, h)),
        out_shape=jax.ShapeDtypeStruct((n, ATTN_W), BF16),
        scratch_shapes=[
            pltpu.VMEM((nsub, 2 * SUB, ATTN_VD), BF16),
            pltpu.VMEM((nsub, 1, 2 * SUB), F32),
            pltpu.VMEM((nsub, ACC_ROWS, 2 * SUB), F32),
            pltpu.VMEM((nsub, TK, 2 * SUB), F32),
            pltpu.VMEM((nsub, TK, 2 * SUB), BF16),
        ],
        compiler_params=pltpu.CompilerParams(
            dimension_semantics=("arbitrary", "arbitrary", "arbitrary"),
            vmem_limit_bytes=VMEM_LIMIT),
        name="diff_attn",
    )(q, k, vt, bias_tiles, lam_p, subln_g)


def _mixer_kernel(x_ref, u_ref, uh_ref, pu_ref, ph_ref, o_ref, gl_ref,
                  dww_ref, dwb_ref, lng_ref, lnb_ref, wco_ref, wao_ref,
                  pw_ref, ps_ref, wpo_ref, wout_ref, gpost_ref,
                  out_ref,
                  sh_ref, pext_ref, cs_ref, pool_ref, merged_ref, *, tiles_per_seq):
    tm = TM_MIX
    t_in_seq = lax.rem(pl.program_id(0), tiles_per_seq)
    keep = t_in_seq > 0

    sh_ref[0, 0:HALO, :] = jnp.where(keep, uh_ref[...].astype(F32), 0.0)
    sh_ref[0, HALO:HALO + tm, :] = u_ref[...].astype(F32)
    span = tm + HALO - SUBLANES
    for r in range(1, SUBLANES):
        sh_ref[r, 0:span, :] = sh_ref[0, r:r + span, :]

    first_off = HALO - (CONV_WIDTH - 1)

    def conv_block(rb, _):
        base = pl.multiple_of(rb * CONV_RB, CONV_RB)
        acc = jnp.zeros((CONV_RB, CONV_CH), F32)
        for j in range(CONV_WIDTH):
            off = first_off + j
            rows = pl.ds(base + (off // SUBLANES) * SUBLANES, CONV_RB)
            acc = acc + sh_ref[off % SUBLANES, rows, :] * dww_ref[j:j + 1, :]
        acc = acc + dwb_ref[...]
        mu = jnp.mean(acc, axis=-1, keepdims=True)
        xc = acc - mu
        var = jnp.mean(xc * xc, axis=-1, keepdims=True)
        y = xc * lax.rsqrt(var + EPS) * lng_ref[...] + lnb_ref[...]
        cs_ref[pl.ds(base, CONV_RB), :] = (y * _sigmoid(y)).astype(BF16)
        return 0

    lax.fori_loop(0, tm // CONV_RB, conv_block, 0)

    pext_ref[0:HALO, :] = jnp.where(keep, ph_ref[...].astype(F32), 0.0)
    pext_ref[HALO:HALO + tm, :] = pu_ref[...].astype(F32)
    pos = t_in_seq * tm + lax.broadcasted_iota(jnp.int32, (tm, POOL_GC), 0)
    for g, w in enumerate(POOL_WINDOWS):
        cols = slice(g * POOL_GC, (g + 1) * POOL_GC)
        e = pext_ref[HALO:HALO + tm, cols]
        tot = e
        for j in range(1, w):
            tot = tot + pext_ref[HALO - j:HALO - j + tm, cols]
        cnt = jnp.minimum(pos + 1, w).astype(F32)
        pooled = tot / cnt - e
        yg = jnp.dot(pooled.astype(BF16), pw_ref[g], preferred_element_type=F32)
        pool_ref[:, cols] = (yg * ps_ref[:, cols]).astype(BF16)

    nc = 256
    for c in range(0, D_MODEL, nc):
        cols = slice(c, c + nc)
        y_conv = jnp.dot(cs_ref[...], wco_ref[:, cols], preferred_element_type=F32)
        y_attn = jnp.dot(o_ref[...], wao_ref[:, cols], preferred_element_type=F32)
        y_pool = jnp.dot(pool_ref[...], wpo_ref[:, cols], preferred_element_type=F32)
        g0 = _sigmoid(gl_ref[:, c:c + nc].astype(F32))
        g1 = _sigmoid(gl_ref[:, D_MODEL + c:D_MODEL + c + nc].astype(F32))
        g2 = _sigmoid(gl_ref[:, 2 * D_MODEL + c:2 * D_MODEL + c + nc].astype(F32))
        merged_ref[:, cols] = (g0 * y_conv + g1 * y_attn + g2 * y_pool).astype(BF16)

    mix = jnp.dot(merged_ref[...], wout_ref[...], preferred_element_type=F32)
    out_ref[...] = x_ref[...] + _rmsnorm(mix, gpost_ref[...])


def _mixer(x2, u, pu, o, gl, dww, dwb, lng, lnb, wco, wao, pw, ps, wpo, wout, gpost, seq):
    n = x2.shape[0]
    tm = TM_MIX
    tiles_per_seq = seq // tm
    row = lambda width: pl.BlockSpec((tm, width), lambda i: (i, 0))
    halo = lambda width: pl.BlockSpec(
        (HALO, width), lambda i: (jnp.maximum(i * (tm // HALO) - 1, 0), 0))
    return pl.pallas_call(
        functools.partial(_mixer_kernel, tiles_per_seq=tiles_per_seq),
        grid=(n // tm,),
        in_specs=[
            row(D_MODEL), row(CONV_CH), halo(CONV_CH), row(POOL_CH), halo(POOL_CH),
            row(ATTN_W), row(N_BRANCH * D_MODEL),
            _resident((CONV_WIDTH, CONV_CH)), _resident((1, CONV_CH)),
            _resident((1, CONV_CH)), _resident((1, CONV_CH)),
            _resident((CONV_CH, D_MODEL)), _resident((ATTN_W, D_MODEL)),
            _resident((len(POOL_WINDOWS), POOL_GC, POOL_GC)), _resident((1, POOL_CH)),
            _resident((POOL_CH, D_MODEL)), _resident((D_MODEL, D_MODEL)),
            _resident((1, D_MODEL)),
        ],
        out_specs=row(D_MODEL),
        out_shape=jax.ShapeDtypeStruct((n, D_MODEL), F32),
        scratch_shapes=[
            pltpu.VMEM((SUBLANES, tm + HALO, CONV_CH), F32),
            pltpu.VMEM((tm + HALO, POOL_CH), F32),
            pltpu.VMEM((tm, CONV_CH), BF16),
            pltpu.VMEM((tm, POOL_CH), BF16),
            pltpu.VMEM((tm, D_MODEL), BF16),
        ],
        compiler_params=pltpu.CompilerParams(
            dimension_semantics=("arbitrary",), vmem_limit_bytes=VMEM_LIMIT),
        name="mixer",
    )(x2, u, u, pu, pu, o, gl, dww, dwb, lng, lnb, wco, wao, pw, ps, wpo, wout, gpost)


def _mlp_kernel(x_ref, p_ref, gpre_ref, w1_ref, w2_ref, gpost_ref, wpp_ref, wpg_ref,
                out_ref, hid_ref):
    x = x_ref[...]
    h = _rmsnorm(x, gpre_ref[...]).astype(BF16)
    fc = 512
    for c in range(0, D_FF, fc):
        a = jnp.dot(h, w1_ref[:, c:c + fc], preferred_element_type=F32)
        a = jnp.maximum(a, 0.0)
        hid_ref[:, c:c + fc] = (a * a).astype(BF16)
    f = jnp.dot(hid_ref[...], w2_ref[...], preferred_element_type=F32)
    x = x + _rmsnorm(f, gpost_ref[...])
    gate = _sigmoid(jnp.dot(x.astype(BF16), wpg_ref[...], preferred_element_type=F32))
    pe = jnp.dot(p_ref[...].astype(BF16), wpp_ref[...], preferred_element_type=F32)
    out_ref[...] = x + gate * pe


def _mlp(x2, p2, gpre, w1, w2, gpost, wpp, wpg):
    n = x2.shape[0]
    tm = TM_MLP
    row = lambda width: pl.BlockSpec((tm, width), lambda i: (i, 0))
    return pl.pallas_call(
        _mlp_kernel,
        grid=(n // tm,),
        in_specs=[
            row(D_MODEL), row(PLE_DIM), _resident((1, D_MODEL)),
            _resident((D_MODEL, D_FF)), _resident((D_FF, D_MODEL)), _resident((1, D_MODEL)),
            _resident((PLE_DIM, D_MODEL)), _resident((D_MODEL, D_MODEL)),
        ],
        out_specs=row(D_MODEL),
        out_shape=jax.ShapeDtypeStruct((n, D_MODEL), F32),
        scratch_shapes=[pltpu.VMEM((tm, D_FF), BF16)],
        compiler_params=pltpu.CompilerParams(
            dimension_semantics=("arbitrary",), vmem_limit_bytes=VMEM_LIMIT),
        name="mlp_ple",
    )(x2, p2, gpre, w1, w2, gpost, wpp, wpg)


def kernel(x, p, rel_bias, g_pre_mix, w_in, conv_dw_w, conv_dw_b, conv_ln_g, conv_ln_b, w_conv_out, lam_p, subln_g, w_attn_out, pool_w, pool_scale, w_pool_out, w_out, g_post_mix, g_pre_mlp, w_mlp_in, w_mlp_out, g_post_mlp, w_ple_proj, w_ple_gate):
    batch, seq, d = x.shape
    depth = w_in.shape[0]
    assert d == D_MODEL and seq % TM_MIX == 0 and seq % TQ == 0 and TQ == TK
    n = batch * seq
    x2 = x.reshape(n, d)
    bias_tiles = _near_bias_tiles(rel_bias)
    row = lambda a: a.reshape(1, -1)
    for i in range(depth):
        lambda_init = 0.8 - 0.6 * math.exp(-0.3 * i)
        w_bf = w_in[i].astype(BF16)
        u, q, k, vt, pu, gl = _proj_in(x2, row(g_pre_mix[i]), w_bf,
                                       w_bf[:, V_COL:V_COL + ATTN_W].T)
        o = _attention(q, k, vt, bias_tiles, lam_p[i], row(subln_g[i]), batch, seq, lambda_init)
        x2 = _mixer(x2, u, pu, o, gl,
                    conv_dw_w[i].reshape(CONV_WIDTH, CONV_CH), row(conv_dw_b[i]),
                    row(conv_ln_g[i]), row(conv_ln_b[i]),
                    w_conv_out[i].astype(BF16), w_attn_out[i].astype(BF16),
                    pool_w[i].astype(BF16), row(pool_scale[i]),
                    w_pool_out[i].astype(BF16), w_out[i].astype(BF16),
                    row(g_post_mix[i]), seq)
        x2 = _mlp(x2, p[i].reshape(n, PLE_DIM), row(g_pre_mlp[i]),
                  w_mlp_in[i].astype(BF16), w_mlp_out[i].astype(BF16), row(g_post_mlp[i]),
                  w_ple_proj[i].astype(BF16), w_ple_gate[i].astype(BF16))
    return x2.reshape(batch, seq, d)
```

```python
import functools
import math

import jax
import jax.numpy as jnp
import numpy as np
from jax import lax
from jax.experimental import pallas as pl
from jax.experimental.pallas import tpu as pltpu

F32 = jnp.float32
BF16 = jnp.bfloat16

D_MODEL = 1024
PLE_DIM = 256
CONV_CH = 512
CONV_WIDTH = 31
ATTN_HEADS = 4
ATTN_HD = 64
ATTN_VD = 2 * ATTN_HD
ATTN_W = ATTN_HEADS * ATTN_VD
POOL_CH = 512
POOL_WINDOWS = (2, 4, 8, 16)
POOL_GC = POOL_CH // len(POOL_WINDOWS)
N_BRANCH = 3
D_FF = 4 * D_MODEL
REL_BUCKETS = 32
REL_MAX_DIST = 128
EPS = 1e-6
MASK_VALUE = -1e30

TM_PROJ = 512
TM_MIX = 512
TM_MLP = 512
TQ = 512
TK = 512
SUB = 128
ONES_ROWS = 16
ACC_ROWS = ATTN_VD + ONES_ROWS
LOG2E = math.log2(math.e)
HALO = 32
CONV_RB = 64
CONV_UNROLL = 2
LANES = 128
VMEM_LIMIT = 56 * 1024 * 1024


def _sigmoid(x):
    return 0.5 * jnp.tanh(0.5 * x) + 0.5


def _rmsnorm(x, g):
    return x * lax.rsqrt(jnp.mean(x * x, axis=-1, keepdims=True) + EPS) * g


def _resident(shape, layer):
    index = (layer,) + (0,) * len(shape)
    return pl.BlockSpec((None,) + tuple(shape), lambda *_: index, pipeline_mode=pl.Buffered(1))


V_COL = 2 * CONV_CH + 2 * ATTN_W


def _proj_in_kernel(x_ref, g_ref, w_ref, wvt_ref, u_ref, q_ref, k_ref, vt_ref, pu_ref, gate_ref):
    h = _rmsnorm(x_ref[...], g_ref[...]).astype(BF16)

    def proj(c0, width):
        return jnp.dot(h, w_ref[:, c0:c0 + width], preferred_element_type=F32)

    ca = proj(0, CONV_CH)
    cb = proj(CONV_CH, CONV_CH)
    u_ref[...] = (ca * _sigmoid(cb)).astype(BF16)
    off = 2 * CONV_CH
    q_ref[...] = (proj(off, ATTN_W) * (ATTN_HD ** -0.5 * LOG2E)).astype(BF16)
    k_ref[...] = proj(off + ATTN_W, ATTN_W).astype(BF16)
    vt_ref[...] = lax.dot_general(wvt_ref[...], h, (((1,), (1,)), ((), ())),
                                  preferred_element_type=F32).astype(BF16)
    pu_ref[...] = proj(V_COL + ATTN_W, POOL_CH).astype(BF16)
    off = V_COL + ATTN_W + POOL_CH
    for c in range(0, N_BRANCH * D_MODEL, 512):
        gate_ref[:, c:c + 512] = _sigmoid(proj(off + c, 512)).astype(BF16)


def _proj_in(x2, g, w_bf, wvt_bf, layer):
    n = x2.shape[0]
    d_in = w_bf.shape[-1]
    row = lambda width: pl.BlockSpec((TM_PROJ, width), lambda i: (i, 0))
    widths = (CONV_CH, ATTN_W, ATTN_W, None, POOL_CH, N_BRANCH * D_MODEL)
    out_specs = [row(w) if w else pl.BlockSpec((ATTN_W, TM_PROJ), lambda i: (0, i)) for w in widths]
    out_shape = [jax.ShapeDtypeStruct((n, w) if w else (ATTN_W, n), BF16) for w in widths]
    return pl.pallas_call(
        _proj_in_kernel,
        grid=(n // TM_PROJ,),
        in_specs=[row(D_MODEL), _resident((1, D_MODEL), layer), _resident((D_MODEL, d_in), layer),
                  _resident((ATTN_W, D_MODEL), layer)],
        out_specs=out_specs,
        out_shape=out_shape,
        compiler_params=pltpu.CompilerParams(
            dimension_semantics=("arbitrary",), vmem_limit_bytes=VMEM_LIMIT),
        name="proj_in",
    )(x2, g, w_bf, wvt_bf)


def _rel_bucket_table(n_max):
    n = np.arange(n_max, dtype=np.int32)
    max_exact = REL_BUCKETS // 2
    nf = np.maximum(n, 1).astype(np.float32)
    large = max_exact + (np.log(nf / max_exact) / np.float32(math.log(REL_MAX_DIST / max_exact))
                         * (REL_BUCKETS - max_exact)).astype(np.int32)
    large = np.minimum(large, REL_BUCKETS - 1)
    return np.where(n < max_exact, n, large)


def _near_bias_tiles(rel_bias):
    assert (_rel_bucket_table(8 * TK)[TK + 1:] == REL_BUCKETS - 1).all()
    heads = rel_bias.shape[1]
    dist = np.arange(-(TK - 1), TK + TQ)
    bucket = _rel_bucket_table(TK + TQ)[np.maximum(dist, 0)]
    shifted = (rel_bias.astype(F32) - rel_bias[REL_BUCKETS - 1].astype(F32)) * LOG2E
    line = jnp.where(dist[None, :] >= 0, shifted[bucket].T, MASK_VALUE)
    span = TK + TQ - 1
    tiles = []
    for d in range(2):
        a = lax.slice_in_dim(line, d * TK, d * TK + span, axis=1)
        a = jnp.pad(a, ((0, 0), (0, 1)))
        m = jnp.tile(a, (1, TK))[:, :TK * span].reshape(heads, TK, span)
        tiles.append(m[:, :, TK - 1:TK - 1 + TQ])
    return jnp.stack(tiles, axis=1)


def _attn_kernel(q_ref, k_ref, vt_ref, bias_ref, lamp_ref, g_ref, o_ref,
                 qb_ref, m_ref, acc_ref, s_ref, p_ref, *, lambda_init):
    nsub = TQ // SUB
    lp = lamp_ref[...]
    lam = (jnp.exp(jnp.sum(lp[0:1] * lp[1:2], axis=-1, keepdims=True))
           - jnp.exp(jnp.sum(lp[2:3] * lp[3:4], axis=-1, keepdims=True)) + lambda_init)

    def q_tile(i, carry):
        _attn_q_tile(i, lam, q_ref, k_ref, vt_ref, bias_ref, g_ref, o_ref,
                     qb_ref, m_ref, acc_ref, s_ref, p_ref, lambda_init)
        return carry

    lax.fori_loop(0, q_ref.shape[0] // TQ, q_tile, 0)


def _attn_q_tile(i, lam, q_ref, k_ref, vt_ref, bias_ref, g_ref, o_ref,
                 qb_ref, m_ref, acc_ref, s_ref, p_ref, lambda_init):
    nsub = TQ // SUB
    q0 = pl.multiple_of(i * TQ, TQ)
    lane = lax.broadcasted_iota(jnp.int32, (SUB, ATTN_VD), 1)
    for c in range(nsub):
        qc = q_ref[pl.ds(q0 + c * SUB, SUB), :]
        zero = jnp.zeros_like(qc)
        qb_ref[c, 0:SUB, :] = jnp.where(lane < ATTN_HD, qc, zero)
        qb_ref[c, SUB:2 * SUB, :] = jnp.where(lane >= ATTN_HD, qc, zero)
        m_ref[c] = jnp.full((1, 2 * SUB), MASK_VALUE, F32)
        acc_ref[c] = jnp.zeros((ACC_ROWS, 2 * SUB), F32)

    def step(ks, n_keys, variant):
        tile_max = []
        for c in range(nsub):
            nk = n_keys[c]
            s = lax.dot_general(k_ref[pl.ds(ks, nk), :], qb_ref[c], (((1,), (1,)), ((), ())),
                                preferred_element_type=F32)
            if variant is not None:
                b = bias_ref[variant, 0:nk, c * SUB:(c + 1) * SUB]
                s = s + jnp.concatenate([b, b], axis=1)
            s_ref[c, 0:nk, :] = s
            tile_max.append(jnp.max(s, axis=0, keepdims=True))
        alphas = []
        for c in range(nsub):
            nk = n_keys[c]
            m_old = m_ref[c]
            m_new = jnp.maximum(m_old, tile_max[c])
            alpha = jnp.exp2(m_old - m_new)
            p_ref[c, 0:nk, :] = jnp.exp2(s_ref[c, 0:nk, :] - m_new).astype(BF16)
            m_ref[c] = m_new
            alphas.append(alpha)
        for c in range(nsub):
            nk = n_keys[c]
            vte = jnp.concatenate([vt_ref[:, pl.ds(ks, nk)], jnp.ones((ONES_ROWS, nk), BF16)], axis=0)
            pv = jnp.dot(vte, p_ref[c, 0:nk, :], preferred_element_type=F32)
            acc_ref[c] = alphas[c] * acc_ref[c] + pv

    full = (TK,) * nsub

    def far(j, carry):
        step(pl.multiple_of(j * TK, TK), full, None)
        return carry

    lax.fori_loop(0, jnp.maximum(i - 1, 0), far, 0)

    @pl.when(i >= 1)
    def _():
        step(pl.multiple_of((i - 1) * TK, TK), full, 1)

    step(pl.multiple_of(i * TK, TK), tuple((c + 1) * SUB for c in range(nsub)), 0)

    for c in range(nsub):
        acc = acc_ref[c, 0:ATTN_VD, :]
        l = acc_ref[c, ATTN_VD:ATTN_VD + 1, :]
        ot = acc[:, :SUB] / l[:, :SUB] - lam * (acc[:, SUB:] / l[:, SUB:])
        o = _rmsnorm(ot.T, g_ref[...]) * (1.0 - lambda_init)
        o_ref[pl.ds(q0 + c * SUB, SUB), :] = o.astype(BF16)


def _attention(q, k, vt, bias_tiles, lam_p, subln_g, batch, seq, layer):
    n = batch * seq
    nsub = TQ // SUB
    lambda_init = 0.8 - 0.6 * math.exp(-0.3 * layer)
    return pl.pallas_call(
        functools.partial(_attn_kernel, lambda_init=lambda_init),
        grid=(batch, ATTN_HEADS),
        in_specs=[
            pl.BlockSpec((seq, ATTN_VD), lambda b, h: (b, h)),
            pl.BlockSpec((seq, ATTN_VD), lambda b, h: (b, h)),
            pl.BlockSpec((ATTN_VD, seq), lambda b, h: (h, b)),
            pl.BlockSpec((None, 2, TK, TQ), lambda b, h: (h, 0, 0, 0)),
            pl.BlockSpec((None, 4, ATTN_HD), lambda b, h: (layer, 0, 0)),
            pl.BlockSpec((None, 1, ATTN_VD), lambda b, h: (layer, 0, 0)),
        ],
        out_specs=pl.BlockSpec((seq, ATTN_VD), lambda b, h: (b, h)),
        out_shape=jax.ShapeDtypeStruct((n, ATTN_W), BF16),
        scratch_shapes=[
            pltpu.VMEM((nsub, 2 * SUB, ATTN_VD), BF16),
            pltpu.VMEM((nsub, 1, 2 * SUB), F32),
            pltpu.VMEM((nsub, ACC_ROWS, 2 * SUB), F32),
            pltpu.VMEM((nsub, TK, 2 * SUB), F32),
            pltpu.VMEM((nsub, TK, 2 * SUB), BF16),
        ],
        compiler_params=pltpu.CompilerParams(
            dimension_semantics=("arbitrary", "arbitrary"),
            vmem_limit_bytes=VMEM_LIMIT),
        name="diff_attn",
    )(q, k, vt, bias_tiles, lam_p, subln_g)


def _mixer_kernel(x_ref, u_ref, uh_ref, pu_ref, ph_ref, o_ref, gate_ref,
                  dww_ref, dwb_ref, lng_ref, lnb_ref, wco_ref, wao_ref,
                  pw_ref, ps_ref, wpo_ref, wout_ref, gpost_ref,
                  out_ref,
                  ext_ref, pext_ref, cs_ref, pool_ref, merged_ref, *, tiles_per_seq):
    tm = TM_MIX
    t_in_seq = lax.rem(pl.program_id(0), tiles_per_seq)
    keep = t_in_seq > 0
    n_slab = CONV_CH // LANES

    def fill(dst_ref, tile_ref, halo_ref):
        for s in range(n_slab):
            cols = slice(s * LANES, (s + 1) * LANES)
            dst_ref[s, 0:HALO, :] = jnp.where(keep, halo_ref[:, cols].astype(F32), 0.0)
            dst_ref[s, HALO:HALO + tm, :] = tile_ref[:, cols].astype(F32)

    fill(ext_ref, u_ref, uh_ref)
    first_off = HALO - (CONV_WIDTH - 1)

    def conv_rows(base):
        conv = []
        for s in range(n_slab):
            cols = slice(s * LANES, (s + 1) * LANES)
            acc = jnp.zeros((CONV_RB, LANES), F32)
            for j in range(CONV_WIDTH):
                acc = acc + ext_ref[s, pl.ds(base + first_off + j, CONV_RB), :] * dww_ref[j:j + 1, cols]
            conv.append(acc + dwb_ref[:, cols])
        mu = sum(jnp.sum(a, axis=-1, keepdims=True) for a in conv) * (1.0 / CONV_CH)
        cen = [a - mu for a in conv]
        var = sum(jnp.sum(a * a, axis=-1, keepdims=True) for a in cen) * (1.0 / CONV_CH)
        inv = lax.rsqrt(var + EPS)
        for s in range(n_slab):
            cols = slice(s * LANES, (s + 1) * LANES)
            y = cen[s] * inv * lng_ref[:, cols] + lnb_ref[:, cols]
            cs_ref[pl.ds(base, CONV_RB), cols] = (y * _sigmoid(y)).astype(BF16)

    def conv_block(it, _):
        for r in range(CONV_UNROLL):
            conv_rows(pl.multiple_of(it * (CONV_UNROLL * CONV_RB) + r * CONV_RB, CONV_RB))
        return 0

    lax.fori_loop(0, tm // (CONV_UNROLL * CONV_RB), conv_block, 0)

    fill(pext_ref, pu_ref, ph_ref)
    pos = t_in_seq * tm + lax.broadcasted_iota(jnp.int32, (tm, POOL_GC), 0)
    for g, w in enumerate(POOL_WINDOWS):
        cols = slice(g * POOL_GC, (g + 1) * POOL_GC)
        e = pext_ref[g, HALO:HALO + tm, :]
        tot = e
        for j in range(1, w):
            tot = tot + pext_ref[g, HALO - j:HALO - j + tm, :]
        cnt = jnp.minimum(pos + 1, w).astype(F32)
        pooled = tot / cnt - e
        yg = jnp.dot(pooled.astype(BF16), pw_ref[g], preferred_element_type=F32)
        pool_ref[:, cols] = (yg * ps_ref[:, cols]).astype(BF16)

    nc = 256
    for c in range(0, D_MODEL, nc):
        cols = slice(c, c + nc)
        y_conv = jnp.dot(cs_ref[...], wco_ref[:, cols], preferred_element_type=F32)
        y_attn = jnp.dot(o_ref[...], wao_ref[:, cols], preferred_element_type=F32)
        y_pool = jnp.dot(pool_ref[...], wpo_ref[:, cols], preferred_element_type=F32)
        g0 = gate_ref[:, c:c + nc].astype(F32)
        g1 = gate_ref[:, D_MODEL + c:D_MODEL + c + nc].astype(F32)
        g2 = gate_ref[:, 2 * D_MODEL + c:2 * D_MODEL + c + nc].astype(F32)
        merged_ref[:, cols] = (g0 * y_conv + g1 * y_attn + g2 * y_pool).astype(BF16)

    mix = jnp.dot(merged_ref[...], wout_ref[...], preferred_element_type=F32)
    out_ref[...] = x_ref[...] + _rmsnorm(mix, gpost_ref[...])


def _mixer(x2, u, pu, o, gates, dww, dwb, lng, lnb, wco, wao, pw, ps, wpo, wout, gpost, seq, layer):
    n = x2.shape[0]
    tm = TM_MIX
    tiles_per_seq = seq // tm
    row = lambda width: pl.BlockSpec((tm, width), lambda i: (i, 0))
    halo = lambda width: pl.BlockSpec(
        (HALO, width), lambda i: (jnp.maximum(i * (tm // HALO) - 1, 0), 0))
    return pl.pallas_call(
        functools.partial(_mixer_kernel, tiles_per_seq=tiles_per_seq),
        grid=(n // tm,),
        in_specs=[
            row(D_MODEL), row(CONV_CH), halo(CONV_CH), row(POOL_CH), halo(POOL_CH),
            row(ATTN_W), row(N_BRANCH * D_MODEL),
            _resident((CONV_WIDTH, CONV_CH), layer), _resident((1, CONV_CH), layer),
            _resident((1, CONV_CH), layer), _resident((1, CONV_CH), layer),
            _resident((CONV_CH, D_MODEL), layer), _resident((ATTN_W, D_MODEL), layer),
            _resident((len(POOL_WINDOWS), POOL_GC, POOL_GC), layer), _resident((1, POOL_CH), layer),
            _resident((POOL_CH, D_MODEL), layer), _resident((D_MODEL, D_MODEL), layer),
            _resident((1, D_MODEL), layer),
        ],
        out_specs=row(D_MODEL),
        out_shape=jax.ShapeDtypeStruct((n, D_MODEL), F32),
        scratch_shapes=[
            pltpu.VMEM((CONV_CH // LANES, tm + HALO, LANES), F32),
            pltpu.VMEM((POOL_CH // LANES, tm + HALO, LANES), F32),
            pltpu.VMEM((tm, CONV_CH), BF16),
            pltpu.VMEM((tm, POOL_CH), BF16),
            pltpu.VMEM((tm, D_MODEL), BF16),
        ],
        compiler_params=pltpu.CompilerParams(
            dimension_semantics=("arbitrary",), vmem_limit_bytes=VMEM_LIMIT),
        name="mixer",
    )(x2, u, u, pu, pu, o, gates, dww, dwb, lng, lnb, wco, wao, pw, ps, wpo, wout, gpost)


def _mlp_kernel(x_ref, p_ref, gpre_ref, w1_ref, w2_ref, gpost_ref, wpp_ref, wpg_ref,
                out_ref, hid_ref):
    x = x_ref[...]
    h = _rmsnorm(x, gpre_ref[...]).astype(BF16)
    fc = 512
    for c in range(0, D_FF, fc):
        a = jnp.dot(h, w1_ref[:, c:c + fc], preferred_element_type=F32)
        a = jnp.maximum(a, 0.0)
        hid_ref[:, c:c + fc] = (a * a).astype(BF16)
    f = jnp.dot(hid_ref[...], w2_ref[...], preferred_element_type=F32)
    x = x + _rmsnorm(f, gpost_ref[...])
    gate = _sigmoid(jnp.dot(x.astype(BF16), wpg_ref[...], preferred_element_type=F32))
    pe = jnp.dot(p_ref[...].astype(BF16), wpp_ref[...], preferred_element_type=F32)
    out_ref[...] = x + gate * pe


def _mlp(x2, p3, gpre, w1, w2, gpost, wpp, wpg, layer):
    n = x2.shape[0]
    tm = TM_MLP
    row = lambda width: pl.BlockSpec((tm, width), lambda i: (i, 0))
    return pl.pallas_call(
        _mlp_kernel,
        grid=(n // tm,),
        in_specs=[
            row(D_MODEL), pl.BlockSpec((None, tm, PLE_DIM), lambda i: (layer, i, 0)),
            _resident((1, D_MODEL), layer),
            _resident((D_MODEL, D_FF), layer), _resident((D_FF, D_MODEL), layer),
            _resident((1, D_MODEL), layer),
            _resident((PLE_DIM, D_MODEL), layer), _resident((D_MODEL, D_MODEL), layer),
        ],
        out_specs=row(D_MODEL),
        out_shape=jax.ShapeDtypeStruct((n, D_MODEL), F32),
        scratch_shapes=[pltpu.VMEM((tm, D_FF), BF16)],
        compiler_params=pltpu.CompilerParams(
            dimension_semantics=("arbitrary",), vmem_limit_bytes=VMEM_LIMIT),
        name="mlp_ple",
    )(x2, p3, gpre, w1, w2, gpost, wpp, wpg)


def kernel(x, p, rel_bias, g_pre_mix, w_in, conv_dw_w, conv_dw_b, conv_ln_g, conv_ln_b, w_conv_out, lam_p, subln_g, w_attn_out, pool_w, pool_scale, w_pool_out, w_out, g_post_mix, g_pre_mlp, w_mlp_in, w_mlp_out, g_post_mlp, w_ple_proj, w_ple_gate):
    batch, seq, d = x.shape
    depth = w_in.shape[0]
    assert d == D_MODEL and seq % TM_MIX == 0 and seq % TQ == 0 and TQ == TK
    n = batch * seq
    x2 = x.reshape(n, d)
    bias_tiles = _near_bias_tiles(rel_bias)
    rows = lambda a: a.reshape(depth, 1, -1)
    bf = lambda a: a.astype(BF16)
    w_in_bf = bf(w_in)
    wvt_bf = jnp.swapaxes(w_in_bf[:, :, V_COL:V_COL + ATTN_W], 1, 2)
    p3 = p.reshape(depth, n, PLE_DIM)
    mixer_params = (conv_dw_w.reshape(depth, CONV_WIDTH, CONV_CH), rows(conv_dw_b),
                    rows(conv_ln_g), rows(conv_ln_b), bf(w_conv_out), bf(w_attn_out),
                    bf(pool_w), rows(pool_scale), bf(w_pool_out), bf(w_out), rows(g_post_mix))
    mlp_params = (rows(g_pre_mlp), bf(w_mlp_in), bf(w_mlp_out), rows(g_post_mlp),
                  bf(w_ple_proj), bf(w_ple_gate))
    g_pre, subln = rows(g_pre_mix), rows(subln_g)
    for layer in range(depth):
        u, q, k, vt, pu, gates = _proj_in(x2, g_pre, w_in_bf, wvt_bf, layer)
        o = _attention(q, k, vt, bias_tiles, lam_p, subln, batch, seq, layer)
        x2 = _mixer(x2, u, pu, o, gates, *mixer_params, seq, layer)
        x2 = _mlp(x2, p3, *mlp_params, layer)
    return x2.reshape(batch, seq, d)
```

```python
import functools
import math

import jax
import jax.numpy as jnp
import numpy as np
from jax import lax
from jax.experimental import pallas as pl
from jax.experimental.pallas import tpu as pltpu

F32 = jnp.float32
BF16 = jnp.bfloat16

D_MODEL = 1024
PLE_DIM = 256
CONV_CH = 512
CONV_WIDTH = 31
ATTN_HEADS = 4
ATTN_HD = 64
ATTN_VD = 2 * ATTN_HD
ATTN_W = ATTN_HEADS * ATTN_VD
POOL_CH = 512
POOL_WINDOWS = (2, 4, 8, 16)
POOL_GC = POOL_CH // len(POOL_WINDOWS)
N_BRANCH = 3
D_FF = 4 * D_MODEL
REL_BUCKETS = 32
REL_MAX_DIST = 128
EPS = 1e-6
MASK_VALUE = -1e30

TM_PROJ = 512
TM_MIX = 512
TM_MLP = 512
TQ = 512
TK = 512
SUB = 128
ONES_ROWS = 16
ACC_ROWS = ATTN_VD + ONES_ROWS
LOG2E = math.log2(math.e)
HALO = 32
CONV_RB = 64
MIX_ROWS = 256
LANES = 128
VMEM_LIMIT = 56 * 1024 * 1024


def _sigmoid(x):
    return 0.5 * jnp.tanh(0.5 * x) + 0.5


def _rmsnorm(x, g):
    return x * lax.rsqrt(jnp.mean(x * x, axis=-1, keepdims=True) + EPS) * g


def _resident(shape, layer):
    index = (layer,) + (0,) * len(shape)
    return pl.BlockSpec((None,) + tuple(shape), lambda *_: index, pipeline_mode=pl.Buffered(1))


V_COL = 2 * CONV_CH + 2 * ATTN_W


def _proj_in_kernel(x_ref, g_ref, w_ref, wvt_ref, u_ref, q_ref, k_ref, vt_ref, pu_ref, gate_ref):
    h = _rmsnorm(x_ref[...], g_ref[...]).astype(BF16)

    def proj(c0, width):
        return jnp.dot(h, w_ref[:, c0:c0 + width], preferred_element_type=F32)

    ca = proj(0, CONV_CH)
    cb = proj(CONV_CH, CONV_CH)
    u_ref[...] = (ca * _sigmoid(cb)).astype(BF16)
    off = 2 * CONV_CH
    q_ref[...] = (proj(off, ATTN_W) * (ATTN_HD ** -0.5 * LOG2E)).astype(BF16)
    k_ref[...] = proj(off + ATTN_W, ATTN_W).astype(BF16)
    vt_ref[...] = lax.dot_general(wvt_ref[...], h, (((1,), (1,)), ((), ())),
                                  preferred_element_type=F32).astype(BF16)
    pu_ref[...] = proj(V_COL + ATTN_W, POOL_CH).astype(BF16)
    off = V_COL + ATTN_W + POOL_CH
    for c in range(0, N_BRANCH * D_MODEL, 512):
        gate_ref[:, c:c + 512] = _sigmoid(proj(off + c, 512)).astype(BF16)


def _proj_in(x2, g, w_bf, wvt_bf, layer):
    n = x2.shape[0]
    d_in = w_bf.shape[-1]
    row = lambda width: pl.BlockSpec((TM_PROJ, width), lambda i: (i, 0))
    widths = (CONV_CH, ATTN_W, ATTN_W, None, POOL_CH, N_BRANCH * D_MODEL)
    out_specs = [row(w) if w else pl.BlockSpec((ATTN_W, TM_PROJ), lambda i: (0, i)) for w in widths]
    out_shape = [jax.ShapeDtypeStruct((n, w) if w else (ATTN_W, n), BF16) for w in widths]
    return pl.pallas_call(
        _proj_in_kernel,
        grid=(n // TM_PROJ,),
        in_specs=[row(D_MODEL), _resident((1, D_MODEL), layer), _resident((D_MODEL, d_in), layer),
                  _resident((ATTN_W, D_MODEL), layer)],
        out_specs=out_specs,
        out_shape=out_shape,
        compiler_params=pltpu.CompilerParams(
            dimension_semantics=("arbitrary",), vmem_limit_bytes=VMEM_LIMIT),
        name="proj_in",
    )(x2, g, w_bf, wvt_bf)


def _rel_bucket_table(n_max):
    n = np.arange(n_max, dtype=np.int32)
    max_exact = REL_BUCKETS // 2
    nf = np.maximum(n, 1).astype(np.float32)
    large = max_exact + (np.log(nf / max_exact) / np.float32(math.log(REL_MAX_DIST / max_exact))
                         * (REL_BUCKETS - max_exact)).astype(np.int32)
    large = np.minimum(large, REL_BUCKETS - 1)
    return np.where(n < max_exact, n, large)


def _near_bias_blocks(rel_bias):
    assert (_rel_bucket_table(8 * TK)[TK + 1:] == REL_BUCKETS - 1).all()
    heads = rel_bias.shape[1]
    dist = np.arange(-(TK - 1), TK + TQ)
    bucket = _rel_bucket_table(TK + TQ)[np.maximum(dist, 0)]
    shifted = (rel_bias.astype(F32) - rel_bias[REL_BUCKETS - 1].astype(F32)) * LOG2E
    line = jnp.where(dist[None, :] >= 0, shifted[bucket].T, MASK_VALUE)
    n_delta = 2 * (TK // SUB) - 1
    win = 2 * SUB - 1
    starts = [d * TK + dp * SUB for d in range(2) for dp in range(n_delta)]
    w = jnp.stack([lax.slice_in_dim(line, s0, s0 + win, axis=1) for s0 in starts], axis=1)
    w = jnp.pad(w, ((0, 0), (0, 0), (0, 1)))
    m = jnp.tile(w, (1, 1, SUB))[:, :, :SUB * win].reshape(heads, 2 * n_delta, SUB, win)
    blocks = m[:, :, :, SUB - 1:2 * SUB - 1]
    return blocks.reshape(heads, 2, n_delta, SUB, SUB)


def _attn_kernel(q_ref, k_ref, vt_ref, bias_ref, lamp_ref, g_ref, o_ref,
                 qb_ref, m_ref, acc_ref, s_ref, p_ref, *, lambda_init):
    nsub = TQ // SUB
    lp = lamp_ref[...]
    lam = (jnp.exp(jnp.sum(lp[0:1] * lp[1:2], axis=-1, keepdims=True))
           - jnp.exp(jnp.sum(lp[2:3] * lp[3:4], axis=-1, keepdims=True)) + lambda_init)

    def q_tile(i, carry):
        _attn_q_tile(i, lam, q_ref, k_ref, vt_ref, bias_ref, g_ref, o_ref,
                     qb_ref, m_ref, acc_ref, s_ref, p_ref, lambda_init)
        return carry

    lax.fori_loop(0, q_ref.shape[0] // TQ, q_tile, 0)


def _attn_q_tile(i, lam, q_ref, k_ref, vt_ref, bias_ref, g_ref, o_ref,
                 qb_ref, m_ref, acc_ref, s_ref, p_ref, lambda_init):
    nsub = TQ // SUB
    q0 = pl.multiple_of(i * TQ, TQ)
    lane = lax.broadcasted_iota(jnp.int32, (SUB, ATTN_VD), 1)
    for c in range(nsub):
        qc = q_ref[pl.ds(q0 + c * SUB, SUB), :]
        zero = jnp.zeros_like(qc)
        qb_ref[c, 0:SUB, :] = jnp.where(lane < ATTN_HD, qc, zero)
        qb_ref[c, SUB:2 * SUB, :] = jnp.where(lane >= ATTN_HD, qc, zero)
        m_ref[c] = jnp.full((1, 2 * SUB), MASK_VALUE, F32)
        acc_ref[c] = jnp.zeros((ACC_ROWS, 2 * SUB), F32)

    def step(ks, n_keys, variant):
        tile_max = []
        for c in range(nsub):
            nk = n_keys[c]
            s = lax.dot_general(k_ref[pl.ds(ks, nk), :], qb_ref[c], (((1,), (1,)), ((), ())),
                                preferred_element_type=F32)
            if variant is not None:
                b = jnp.concatenate([bias_ref[variant, c - bk + TK // SUB - 1]
                                     for bk in range(nk // SUB)], axis=0)
                s = s + jnp.concatenate([b, b], axis=1)
            s_ref[c, 0:nk, :] = s
            tile_max.append(jnp.max(s, axis=0, keepdims=True))
        alphas = []
        for c in range(nsub):
            nk = n_keys[c]
            m_old = m_ref[c]
            m_new = jnp.maximum(m_old, tile_max[c])
            alpha = jnp.exp2(m_old - m_new)
            p_ref[c, 0:nk, :] = jnp.exp2(s_ref[c, 0:nk, :] - m_new).astype(BF16)
            m_ref[c] = m_new
            alphas.append(alpha)
        for c in range(nsub):
            nk = n_keys[c]
            vte = jnp.concatenate([vt_ref[:, pl.ds(ks, nk)], jnp.ones((ONES_ROWS, nk), BF16)], axis=0)
            pv = jnp.dot(vte, p_ref[c, 0:nk, :], preferred_element_type=F32)
            acc_ref[c] = alphas[c] * acc_ref[c] + pv

    full = (TK,) * nsub

    def far(j, carry):
        step(pl.multiple_of(j * TK, TK), full, None)
        return carry

    lax.fori_loop(0, jnp.maximum(i - 1, 0), far, 0)

    @pl.when(i >= 1)
    def _():
        step(pl.multiple_of((i - 1) * TK, TK), full, 1)

    step(pl.multiple_of(i * TK, TK), tuple((c + 1) * SUB for c in range(nsub)), 0)

    for c in range(nsub):
        acc = acc_ref[c, 0:ATTN_VD, :]
        l = acc_ref[c, ATTN_VD:ATTN_VD + 1, :]
        ot = acc[:, :SUB] / l[:, :SUB] - lam * (acc[:, SUB:] / l[:, SUB:])
        o = _rmsnorm(ot.T, g_ref[...]) * (1.0 - lambda_init)
        o_ref[pl.ds(q0 + c * SUB, SUB), :] = o.astype(BF16)


def _attention(q, k, vt, bias_blocks, lam_p, subln_g, batch, seq, layer):
    n = batch * seq
    nsub = TQ // SUB
    lambda_init = 0.8 - 0.6 * math.exp(-0.3 * layer)
    return pl.pallas_call(
        functools.partial(_attn_kernel, lambda_init=lambda_init),
        grid=(batch, ATTN_HEADS),
        in_specs=[
            pl.BlockSpec((seq, ATTN_VD), lambda b, h: (b, h)),
            pl.BlockSpec((seq, ATTN_VD), lambda b, h: (b, h)),
            pl.BlockSpec((ATTN_VD, seq), lambda b, h: (h, b)),
            pl.BlockSpec((None, 2, 2 * (TK // SUB) - 1, SUB, SUB), lambda b, h: (h, 0, 0, 0, 0)),
            pl.BlockSpec((None, 4, ATTN_HD), lambda b, h: (layer, 0, 0)),
            pl.BlockSpec((None, 1, ATTN_VD), lambda b, h: (layer, 0, 0)),
        ],
        out_specs=pl.BlockSpec((seq, ATTN_VD), lambda b, h: (b, h)),
        out_shape=jax.ShapeDtypeStruct((n, ATTN_W), BF16),
        scratch_shapes=[
            pltpu.VMEM((nsub, 2 * SUB, ATTN_VD), BF16),
            pltpu.VMEM((nsub, 1, 2 * SUB), F32),
            pltpu.VMEM((nsub, ACC_ROWS, 2 * SUB), F32),
            pltpu.VMEM((nsub, TK, 2 * SUB), F32),
            pltpu.VMEM((nsub, TK, 2 * SUB), BF16),
        ],
        compiler_params=pltpu.CompilerParams(
            dimension_semantics=("arbitrary", "arbitrary"),
            vmem_limit_bytes=VMEM_LIMIT),
        name="diff_attn",
    )(q, k, vt, bias_blocks, lam_p, subln_g)


def _mixer_kernel(x_ref, u_ref, uh_ref, pu_ref, ph_ref, o_ref, gate_ref,
                  dww_ref, dwb_ref, lng_ref, lnb_ref, wco_ref, wao_ref,
                  pw_ref, ps_ref, wpo_ref, wout_ref, gpost_ref,
                  out_ref,
                  ext_ref, pext_ref, cs_ref, pool_ref, merged_ref, *, tiles_per_seq):
    tm = TM_MIX
    t_in_seq = lax.rem(pl.program_id(0), tiles_per_seq)
    keep = t_in_seq > 0
    n_slab = CONV_CH // LANES

    def fill(dst_ref, tile_ref, halo_ref):
        for s in range(n_slab):
            cols = slice(s * LANES, (s + 1) * LANES)
            dst_ref[s, 0:HALO, :] = jnp.where(keep, halo_ref[:, cols].astype(F32), 0.0)
            dst_ref[s, HALO:HALO + tm, :] = tile_ref[:, cols].astype(F32)

    fill(ext_ref, u_ref, uh_ref)
    first_off = HALO - (CONV_WIDTH - 1)

    def conv_rows(base):
        conv = []
        for s in range(n_slab):
            cols = slice(s * LANES, (s + 1) * LANES)
            acc = jnp.zeros((CONV_RB, LANES), F32)
            for j in range(CONV_WIDTH):
                acc = acc + ext_ref[s, pl.ds(base + first_off + j, CONV_RB), :] * dww_ref[j:j + 1, cols]
            conv.append(acc + dwb_ref[:, cols])
        mu = sum(jnp.sum(a, axis=-1, keepdims=True) for a in conv) * (1.0 / CONV_CH)
        cen = [a - mu for a in conv]
        var = sum(jnp.sum(a * a, axis=-1, keepdims=True) for a in cen) * (1.0 / CONV_CH)
        inv = lax.rsqrt(var + EPS)
        for s in range(n_slab):
            cols = slice(s * LANES, (s + 1) * LANES)
            y = cen[s] * inv * lng_ref[:, cols] + lnb_ref[:, cols]
            cs_ref[pl.ds(base, CONV_RB), cols] = (y * _sigmoid(y)).astype(BF16)

    fill(pext_ref, pu_ref, ph_ref)
    pos = t_in_seq * tm + lax.broadcasted_iota(jnp.int32, (tm, POOL_GC), 0)
    for g, w in enumerate(POOL_WINDOWS):
        cols = slice(g * POOL_GC, (g + 1) * POOL_GC)
        e = pext_ref[g, HALO:HALO + tm, :]
        tot = e
        for j in range(1, w):
            tot = tot + pext_ref[g, HALO - j:HALO - j + tm, :]
        cnt = jnp.minimum(pos + 1, w).astype(F32)
        pooled = tot / cnt - e
        yg = jnp.dot(pooled.astype(BF16), pw_ref[g], preferred_element_type=F32)
        pool_ref[:, cols] = (yg * ps_ref[:, cols]).astype(BF16)

    nc = 256
    for r0 in range(0, tm, MIX_ROWS):
        rows = slice(r0, r0 + MIX_ROWS)
        for base in range(r0, r0 + MIX_ROWS, CONV_RB):
            conv_rows(base)
        for c in range(0, D_MODEL, nc):
            cols = slice(c, c + nc)
            y_conv = jnp.dot(cs_ref[rows, :], wco_ref[:, cols], preferred_element_type=F32)
            y_attn = jnp.dot(o_ref[rows, :], wao_ref[:, cols], preferred_element_type=F32)
            y_pool = jnp.dot(pool_ref[rows, :], wpo_ref[:, cols], preferred_element_type=F32)
            g0 = gate_ref[rows, c:c + nc].astype(F32)
            g1 = gate_ref[rows, D_MODEL + c:D_MODEL + c + nc].astype(F32)
            g2 = gate_ref[rows, 2 * D_MODEL + c:2 * D_MODEL + c + nc].astype(F32)
            merged_ref[rows, cols] = (g0 * y_conv + g1 * y_attn + g2 * y_pool).astype(BF16)
        mix = jnp.dot(merged_ref[rows, :], wout_ref[...], preferred_element_type=F32)
        out_ref[rows, :] = x_ref[rows, :] + _rmsnorm(mix, gpost_ref[...])


def _mixer(x2, u, pu, o, gates, dww, dwb, lng, lnb, wco, wao, pw, ps, wpo, wout, gpost, seq, layer):
    n = x2.shape[0]
    tm = TM_MIX
    tiles_per_seq = seq // tm
    row = lambda width: pl.BlockSpec((tm, width), lambda i: (i, 0))
    halo = lambda width: pl.BlockSpec(
        (HALO, width), lambda i: (jnp.maximum(i * (tm // HALO) - 1, 0), 0))
    return pl.pallas_call(
        functools.partial(_mixer_kernel, tiles_per_seq=tiles_per_seq),
        grid=(n // tm,),
        in_specs=[
            row(D_MODEL), row(CONV_CH), halo(CONV_CH), row(POOL_CH), halo(POOL_CH),
            row(ATTN_W), row(N_BRANCH * D_MODEL),
            _resident((CONV_WIDTH, CONV_CH), layer), _resident((1, CONV_CH), layer),
            _resident((1, CONV_CH), layer), _resident((1, CONV_CH), layer),
            _resident((CONV_CH, D_MODEL), layer), _resident((ATTN_W, D_MODEL), layer),
            _resident((len(POOL_WINDOWS), POOL_GC, POOL_GC), layer), _resident((1, POOL_CH), layer),
            _resident((POOL_CH, D_MODEL), layer), _resident((D_MODEL, D_MODEL), layer),
            _resident((1, D_MODEL), layer),
        ],
        out_specs=row(D_MODEL),
        out_shape=jax.ShapeDtypeStruct((n, D_MODEL), F32),
        scratch_shapes=[
            pltpu.VMEM((CONV_CH // LANES, tm + HALO, LANES), F32),
            pltpu.VMEM((POOL_CH // LANES, tm + HALO, LANES), F32),
            pltpu.VMEM((tm, CONV_CH), BF16),
            pltpu.VMEM((tm, POOL_CH), BF16),
            pltpu.VMEM((tm, D_MODEL), BF16),
        ],
        compiler_params=pltpu.CompilerParams(
            dimension_semantics=("arbitrary",), vmem_limit_bytes=VMEM_LIMIT),
        name="mixer",
    )(x2, u, u, pu, pu, o, gates, dww, dwb, lng, lnb, wco, wao, pw, ps, wpo, wout, gpost)


def _mlp_kernel(x_ref, p_ref, gpre_ref, w1_ref, w2_ref, gpost_ref, wpp_ref, wpg_ref,
                out_ref, hid_ref):
    x = x_ref[...]
    h = _rmsnorm(x, gpre_ref[...]).astype(BF16)
    fc = 512
    for c in range(0, D_FF, fc):
        a = jnp.dot(h, w1_ref[:, c:c + fc], preferred_element_type=F32)
        a = jnp.maximum(a, 0.0)
        hid_ref[:, c:c + fc] = (a * a).astype(BF16)
    f = jnp.dot(hid_ref[...], w2_ref[...], preferred_element_type=F32)
    x = x + _rmsnorm(f, gpost_ref[...])
    gate = _sigmoid(jnp.dot(x.astype(BF16), wpg_ref[...], preferred_element_type=F32))
    pe = jnp.dot(p_ref[...].astype(BF16), wpp_ref[...], preferred_element_type=F32)
    out_ref[...] = x + gate * pe


def _mlp(x2, p3, gpre, w1, w2, gpost, wpp, wpg, layer):
    n = x2.shape[0]
    tm = TM_MLP
    row = lambda width: pl.BlockSpec((tm, width), lambda i: (i, 0))
    return pl.pallas_call(
        _mlp_kernel,
        grid=(n // tm,),
        in_specs=[
            row(D_MODEL), pl.BlockSpec((None, tm, PLE_DIM), lambda i: (layer, i, 0)),
            _resident((1, D_MODEL), layer),
            _resident((D_MODEL, D_FF), layer), _resident((D_FF, D_MODEL), layer),
            _resident((1, D_MODEL), layer),
            _resident((PLE_DIM, D_MODEL), layer), _resident((D_MODEL, D_MODEL), layer),
        ],
        out_specs=row(D_MODEL),
        out_shape=jax.ShapeDtypeStruct((n, D_MODEL), F32),
        scratch_shapes=[pltpu.VMEM((tm, D_FF), BF16)],
        compiler_params=pltpu.CompilerParams(
            dimension_semantics=("arbitrary",), vmem_limit_bytes=VMEM_LIMIT),
        name="mlp_ple",
    )(x2, p3, gpre, w1, w2, gpost, wpp, wpg)


def kernel(x, p, rel_bias, g_pre_mix, w_in, conv_dw_w, conv_dw_b, conv_ln_g, conv_ln_b, w_conv_out, lam_p, subln_g, w_attn_out, pool_w, pool_scale, w_pool_out, w_out, g_post_mix, g_pre_mlp, w_mlp_in, w_mlp_out, g_post_mlp, w_ple_proj, w_ple_gate):
    batch, seq, d = x.shape
    depth = w_in.shape[0]
    assert d == D_MODEL and seq % TM_MIX == 0 and seq % TQ == 0 and TQ == TK
    n = batch * seq
    x2 = x.reshape(n, d)
    bias_blocks = _near_bias_blocks(rel_bias)
    rows = lambda a: a.reshape(depth, 1, -1)
    bf = lambda a: a.astype(BF16)
    w_in_bf = bf(w_in)
    wvt_bf = jnp.swapaxes(w_in_bf[:, :, V_COL:V_COL + ATTN_W], 1, 2)
    p3 = p.reshape(depth, n, PLE_DIM)
    mixer_params = (conv_dw_w.reshape(depth, CONV_WIDTH, CONV_CH), rows(conv_dw_b),
                    rows(conv_ln_g), rows(conv_ln_b), bf(w_conv_out), bf(w_attn_out),
                    bf(pool_w), rows(pool_scale), bf(w_pool_out), bf(w_out), rows(g_post_mix))
    mlp_params = (rows(g_pre_mlp), bf(w_mlp_in), bf(w_mlp_out), rows(g_post_mlp),
                  bf(w_ple_proj), bf(w_ple_gate))
    g_pre, subln = rows(g_pre_mix), rows(subln_g)
    for layer in range(depth):
        u, q, k, vt, pu, gates = _proj_in(x2, g_pre, w_in_bf, wvt_bf, layer)
        o = _attention(q, k, vt, bias_blocks, lam_p, subln, batch, seq, layer)
        x2 = _mixer(x2, u, pu, o, gates, *mixer_params, seq, layer)
        x2 = _mlp(x2, p3, *mlp_params, layer)
    return x2.reshape(batch, seq, d)
```

```python
import functools
import math

import jax
import jax.numpy as jnp
import numpy as np
from jax import lax
from jax.experimental import pallas as pl
from jax.experimental.pallas import tpu as pltpu

F32 = jnp.float32
BF16 = jnp.bfloat16

D_MODEL = 1024
PLE_DIM = 256
CONV_CH = 512
CONV_WIDTH = 31
ATTN_HEADS = 4
ATTN_HD = 64
ATTN_VD = 2 * ATTN_HD
ATTN_W = ATTN_HEADS * ATTN_VD
POOL_CH = 512
POOL_WINDOWS = (2, 4, 8, 16)
POOL_GC = POOL_CH // len(POOL_WINDOWS)
N_BRANCH = 3
D_FF = 4 * D_MODEL
REL_BUCKETS = 32
REL_MAX_DIST = 128
EPS = 1e-6
MASK_VALUE = -1e30

TM_PROJ = 512
TM_MIX = 512
TM_MLP = 512
TQ = 512
TK = 512
SUB = 128
HEADS_PER_STEP = 2
ONES_ROWS = 16
ACC_ROWS = ATTN_VD + ONES_ROWS
LOG2E = math.log2(math.e)
HALO = 32
CONV_RB = 64
MIX_ROWS = 256
LANES = 128
VMEM_LIMIT = 56 * 1024 * 1024


def _sigmoid(x):
    return 0.5 * jnp.tanh(0.5 * x) + 0.5


def _rmsnorm(x, g):
    return x * lax.rsqrt(jnp.mean(x * x, axis=-1, keepdims=True) + EPS) * g


def _resident(shape, layer):
    index = (layer,) + (0,) * len(shape)
    return pl.BlockSpec((None,) + tuple(shape), lambda *_: index, pipeline_mode=pl.Buffered(1))


V_COL = 2 * CONV_CH + 2 * ATTN_W


def _proj_in_kernel(x_ref, g_ref, w_ref, wvt_ref, u_ref, q_ref, k_ref, vt_ref, pu_ref, gate_ref):
    h = _rmsnorm(x_ref[...], g_ref[...]).astype(BF16)

    def proj(c0, width):
        return jnp.dot(h, w_ref[:, c0:c0 + width], preferred_element_type=F32)

    ca = proj(0, CONV_CH)
    cb = proj(CONV_CH, CONV_CH)
    u_ref[...] = (ca * _sigmoid(cb)).astype(BF16)
    off = 2 * CONV_CH
    q_ref[...] = (proj(off, ATTN_W) * (ATTN_HD ** -0.5 * LOG2E)).astype(BF16)
    k_ref[...] = proj(off + ATTN_W, ATTN_W).astype(BF16)
    vt_ref[...] = lax.dot_general(wvt_ref[...], h, (((1,), (1,)), ((), ())),
                                  preferred_element_type=F32).astype(BF16)
    pu_ref[...] = proj(V_COL + ATTN_W, POOL_CH).astype(BF16)
    off = V_COL + ATTN_W + POOL_CH
    for c in range(0, N_BRANCH * D_MODEL, 512):
        gate_ref[:, c:c + 512] = _sigmoid(proj(off + c, 512)).astype(BF16)


def _proj_in(x2, g, w_bf, wvt_bf, layer):
    n = x2.shape[0]
    d_in = w_bf.shape[-1]
    row = lambda width: pl.BlockSpec((TM_PROJ, width), lambda i: (i, 0))
    widths = (CONV_CH, ATTN_W, ATTN_W, None, POOL_CH, N_BRANCH * D_MODEL)
    out_specs = [row(w) if w else pl.BlockSpec((ATTN_W, TM_PROJ), lambda i: (0, i)) for w in widths]
    out_shape = [jax.ShapeDtypeStruct((n, w) if w else (ATTN_W, n), BF16) for w in widths]
    return pl.pallas_call(
        _proj_in_kernel,
        grid=(n // TM_PROJ,),
        in_specs=[row(D_MODEL), _resident((1, D_MODEL), layer), _resident((D_MODEL, d_in), layer),
                  _resident((ATTN_W, D_MODEL), layer)],
        out_specs=out_specs,
        out_shape=out_shape,
        compiler_params=pltpu.CompilerParams(
            dimension_semantics=("arbitrary",), vmem_limit_bytes=VMEM_LIMIT),
        name="proj_in",
    )(x2, g, w_bf, wvt_bf)


def _rel_bucket_table(n_max):
    n = np.arange(n_max, dtype=np.int32)
    max_exact = REL_BUCKETS // 2
    nf = np.maximum(n, 1).astype(np.float32)
    large = max_exact + (np.log(nf / max_exact) / np.float32(math.log(REL_MAX_DIST / max_exact))
                         * (REL_BUCKETS - max_exact)).astype(np.int32)
    large = np.minimum(large, REL_BUCKETS - 1)
    return np.where(n < max_exact, n, large)


def _near_bias_blocks(rel_bias):
    assert (_rel_bucket_table(8 * TK)[TK + 1:] == REL_BUCKETS - 1).all()
    heads = rel_bias.shape[1]
    dist = np.arange(-(TK - 1), TK + TQ)
    bucket = _rel_bucket_table(TK + TQ)[np.maximum(dist, 0)]
    shifted = (rel_bias.astype(F32) - rel_bias[REL_BUCKETS - 1].astype(F32)) * LOG2E
    line = jnp.where(dist[None, :] >= 0, shifted[bucket].T, MASK_VALUE)
    n_delta = 2 * (TK // SUB) - 1
    win = 2 * SUB - 1
    starts = [d * TK + dp * SUB for d in range(2) for dp in range(n_delta)]
    w = jnp.stack([lax.slice_in_dim(line, s0, s0 + win, axis=1) for s0 in starts], axis=1)
    w = jnp.pad(w, ((0, 0), (0, 0), (0, 1)))
    m = jnp.tile(w, (1, 1, SUB))[:, :, :SUB * win].reshape(heads, 2 * n_delta, SUB, win)
    blocks = m[:, :, :, SUB - 1:2 * SUB - 1]
    return blocks.reshape(heads, 2, n_delta, SUB, SUB)


def _attn_kernel(q_ref, k_ref, vt_ref, bias_ref, lamp_ref, g_ref, o_ref,
                 qb_ref, m_ref, acc_ref, s_ref, p_ref, *, lambda_init):
    nsub = TQ // SUB
    lp = lamp_ref[...]
    lam = (jnp.exp(jnp.sum(lp[0:1] * lp[1:2], axis=-1, keepdims=True))
           - jnp.exp(jnp.sum(lp[2:3] * lp[3:4], axis=-1, keepdims=True)) + lambda_init)

    def q_tile(i, carry):
        _attn_q_tile(i, lam, q_ref, k_ref, vt_ref, bias_ref, g_ref, o_ref,
                     qb_ref, m_ref, acc_ref, s_ref, p_ref, lambda_init)
        return carry

    lax.fori_loop(0, q_ref.shape[0] // TQ, q_tile, 0)


def _attn_q_tile(i, lam, q_ref, k_ref, vt_ref, bias_ref, g_ref, o_ref,
                 qb_ref, m_ref, acc_ref, s_ref, p_ref, lambda_init):
    nsub = TQ // SUB
    chains = [(ch, ch // nsub, ch % nsub) for ch in range(HEADS_PER_STEP * nsub)]
    hcols = lambda hh: slice(hh * ATTN_VD, (hh + 1) * ATTN_VD)
    q0 = pl.multiple_of(i * TQ, TQ)
    lane = lax.broadcasted_iota(jnp.int32, (SUB, ATTN_VD), 1)
    for ch, hh, c in chains:
        qc = q_ref[pl.ds(q0 + c * SUB, SUB), hcols(hh)]
        zero = jnp.zeros_like(qc)
        qb_ref[ch, 0:SUB, :] = jnp.where(lane < ATTN_HD, qc, zero)
        qb_ref[ch, SUB:2 * SUB, :] = jnp.where(lane >= ATTN_HD, qc, zero)
        m_ref[ch] = jnp.full((1, 2 * SUB), MASK_VALUE, F32)
        acc_ref[ch] = jnp.zeros((ACC_ROWS, 2 * SUB), F32)

    def step(ks, n_keys, variant):
        tile_max = []
        for ch, hh, c in chains:
            nk = n_keys[c]
            s = lax.dot_general(k_ref[pl.ds(ks, nk), hcols(hh)], qb_ref[ch], (((1,), (1,)), ((), ())),
                                preferred_element_type=F32)
            if variant is not None:
                b = jnp.concatenate([bias_ref[hh, variant, c - bk + TK // SUB - 1]
                                     for bk in range(nk // SUB)], axis=0)
                s = s + jnp.concatenate([b, b], axis=1)
            s_ref[ch, 0:nk, :] = s
            tile_max.append(jnp.max(s, axis=0, keepdims=True))
        alphas = []
        for ch, hh, c in chains:
            nk = n_keys[c]
            m_old = m_ref[ch]
            m_new = jnp.maximum(m_old, tile_max[ch])
            alpha = jnp.exp2(m_old - m_new)
            p_ref[ch, 0:nk, :] = jnp.exp2(s_ref[ch, 0:nk, :] - m_new).astype(BF16)
            m_ref[ch] = m_new
            alphas.append(alpha)
        for ch, hh, c in chains:
            nk = n_keys[c]
            vte = jnp.concatenate([vt_ref[hcols(hh), pl.ds(ks, nk)], jnp.ones((ONES_ROWS, nk), BF16)],
                                  axis=0)
            pv = jnp.dot(vte, p_ref[ch, 0:nk, :], preferred_element_type=F32)
            acc_ref[ch] = alphas[ch] * acc_ref[ch] + pv

    full = (TK,) * nsub

    def far(j, carry):
        step(pl.multiple_of(j * TK, TK), full, None)
        return carry

    lax.fori_loop(0, jnp.maximum(i - 1, 0), far, 0)

    @pl.when(i >= 1)
    def _():
        step(pl.multiple_of((i - 1) * TK, TK), full, 1)

    step(pl.multiple_of(i * TK, TK), tuple((c + 1) * SUB for c in range(nsub)), 0)

    for ch, hh, c in chains:
        acc = acc_ref[ch, 0:ATTN_VD, :]
        l = acc_ref[ch, ATTN_VD:ATTN_VD + 1, :]
        ot = acc[:, :SUB] / l[:, :SUB] - lam * (acc[:, SUB:] / l[:, SUB:])
        o = _rmsnorm(ot.T, g_ref[...]) * (1.0 - lambda_init)
        o_ref[pl.ds(q0 + c * SUB, SUB), hcols(hh)] = o.astype(BF16)


def _attention(q, k, vt, bias_blocks, lam_p, subln_g, batch, seq, layer):
    n = batch * seq
    nchain = HEADS_PER_STEP * (TQ // SUB)
    hw = HEADS_PER_STEP * ATTN_VD
    lambda_init = 0.8 - 0.6 * math.exp(-0.3 * layer)
    return pl.pallas_call(
        functools.partial(_attn_kernel, lambda_init=lambda_init),
        grid=(batch, ATTN_HEADS // HEADS_PER_STEP),
        in_specs=[
            pl.BlockSpec((seq, hw), lambda b, h: (b, h)),
            pl.BlockSpec((seq, hw), lambda b, h: (b, h)),
            pl.BlockSpec((hw, seq), lambda b, h: (h, b)),
            pl.BlockSpec((HEADS_PER_STEP, 2, 2 * (TK // SUB) - 1, SUB, SUB),
                         lambda b, h: (h, 0, 0, 0, 0)),
            pl.BlockSpec((None, 4, ATTN_HD), lambda b, h: (layer, 0, 0)),
            pl.BlockSpec((None, 1, ATTN_VD), lambda b, h: (layer, 0, 0)),
        ],
        out_specs=pl.BlockSpec((seq, hw), lambda b, h: (b, h)),
        out_shape=jax.ShapeDtypeStruct((n, ATTN_W), BF16),
        scratch_shapes=[
            pltpu.VMEM((nchain, 2 * SUB, ATTN_VD), BF16),
            pltpu.VMEM((nchain, 1, 2 * SUB), F32),
            pltpu.VMEM((nchain, ACC_ROWS, 2 * SUB), F32),
            pltpu.VMEM((nchain, TK, 2 * SUB), F32),
            pltpu.VMEM((nchain, TK, 2 * SUB), BF16),
        ],
        compiler_params=pltpu.CompilerParams(
            dimension_semantics=("arbitrary", "arbitrary"),
            vmem_limit_bytes=VMEM_LIMIT),
        name="diff_attn",
    )(q, k, vt, bias_blocks, lam_p, subln_g)


def _mixer_kernel(x_ref, u_ref, uh_ref, pu_ref, ph_ref, o_ref, gate_ref,
                  dww_ref, dwb_ref, lng_ref, lnb_ref, wco_ref, wao_ref,
                  pw_ref, ps_ref, wpo_ref, wout_ref, gpost_ref,
                  out_ref,
                  ext_ref, pext_ref, cs_ref, pool_ref, merged_ref, *, tiles_per_seq):
    tm = TM_MIX
    t_in_seq = lax.rem(pl.program_id(0), tiles_per_seq)
    keep = t_in_seq > 0
    n_slab = CONV_CH // LANES

    def fill(dst_ref, tile_ref, halo_ref):
        for s in range(n_slab):
            cols = slice(s * LANES, (s + 1) * LANES)
            dst_ref[s, 0:HALO, :] = jnp.where(keep, halo_ref[:, cols].astype(F32), 0.0)
            dst_ref[s, HALO:HALO + tm, :] = tile_ref[:, cols].astype(F32)

    fill(ext_ref, u_ref, uh_ref)
    first_off = HALO - (CONV_WIDTH - 1)

    def conv_rows(base):
        conv = []
        for s in range(n_slab):
            cols = slice(s * LANES, (s + 1) * LANES)
            acc = jnp.zeros((CONV_RB, LANES), F32)
            for j in range(CONV_WIDTH):
                acc = acc + ext_ref[s, pl.ds(base + first_off + j, CONV_RB), :] * dww_ref[j:j + 1, cols]
            conv.append(acc + dwb_ref[:, cols])
        mu = sum(jnp.sum(a, axis=-1, keepdims=True) for a in conv) * (1.0 / CONV_CH)
        cen = [a - mu for a in conv]
        var = sum(jnp.sum(a * a, axis=-1, keepdims=True) for a in cen) * (1.0 / CONV_CH)
        inv = lax.rsqrt(var + EPS)
        for s in range(n_slab):
            cols = slice(s * LANES, (s + 1) * LANES)
            y = cen[s] * inv * lng_ref[:, cols] + lnb_ref[:, cols]
            cs_ref[pl.ds(base, CONV_RB), cols] = (y * _sigmoid(y)).astype(BF16)

    fill(pext_ref, pu_ref, ph_ref)
    pos = t_in_seq * tm + lax.broadcasted_iota(jnp.int32, (tm, POOL_GC), 0)
    for g, w in enumerate(POOL_WINDOWS):
        cols = slice(g * POOL_GC, (g + 1) * POOL_GC)
        e = pext_ref[g, HALO:HALO + tm, :]
        tot = e
        for j in range(1, w):
            tot = tot + pext_ref[g, HALO - j:HALO - j + tm, :]
        cnt = jnp.minimum(pos + 1, w).astype(F32)
        pooled = tot / cnt - e
        yg = jnp.dot(pooled.astype(BF16), pw_ref[g], preferred_element_type=F32)
        pool_ref[:, cols] = (yg * ps_ref[:, cols]).astype(BF16)

    nc = 256
    for r0 in range(0, tm, MIX_ROWS):
        rows = slice(r0, r0 + MIX_ROWS)
        for base in range(r0, r0 + MIX_ROWS, CONV_RB):
            conv_rows(base)
        for c in range(0, D_MODEL, nc):
            cols = slice(c, c + nc)
            y_conv = jnp.dot(cs_ref[rows, :], wco_ref[:, cols], preferred_element_type=F32)
            y_attn = jnp.dot(o_ref[rows, :], wao_ref[:, cols], preferred_element_type=F32)
            y_pool = jnp.dot(pool_ref[rows, :], wpo_ref[:, cols], preferred_element_type=F32)
            g0 = gate_ref[rows, c:c + nc].astype(F32)
            g1 = gate_ref[rows, D_MODEL + c:D_MODEL + c + nc].astype(F32)
            g2 = gate_ref[rows, 2 * D_MODEL + c:2 * D_MODEL + c + nc].astype(F32)
            merged_ref[rows, cols] = (g0 * y_conv + g1 * y_attn + g2 * y_pool).astype(BF16)
        mix = jnp.dot(merged_ref[rows, :], wout_ref[...], preferred_element_type=F32)
        out_ref[rows, :] = x_ref[rows, :] + _rmsnorm(mix, gpost_ref[...])


def _mixer(x2, u, pu, o, gates, dww, dwb, lng, lnb, wco, wao, pw, ps, wpo, wout, gpost, seq, layer):
    n = x2.shape[0]
    tm = TM_MIX
    tiles_per_seq = seq // tm
    row = lambda width: pl.BlockSpec((tm, width), lambda i: (i, 0))
    halo = lambda width: pl.BlockSpec(
        (HALO, width), lambda i: (jnp.maximum(i * (tm // HALO) - 1, 0), 0))
    return pl.pallas_call(
        functools.partial(_mixer_kernel, tiles_per_seq=tiles_per_seq),
        grid=(n // tm,),
        in_specs=[
            row(D_MODEL), row(CONV_CH), halo(CONV_CH), row(POOL_CH), halo(POOL_CH),
            row(ATTN_W), row(N_BRANCH * D_MODEL),
            _resident((CONV_WIDTH, CONV_CH), layer), _resident((1, CONV_CH), layer),
            _resident((1, CONV_CH), layer), _resident((1, CONV_CH), layer),
            _resident((CONV_CH, D_MODEL), layer), _resident((ATTN_W, D_MODEL), layer),
            _resident((len(POOL_WINDOWS), POOL_GC, POOL_GC), layer), _resident((1, POOL_CH), layer),
            _resident((POOL_CH, D_MODEL), layer), _resident((D_MODEL, D_MODEL), layer),
            _resident((1, D_MODEL), layer),
        ],
        out_specs=row(D_MODEL),
        out_shape=jax.ShapeDtypeStruct((n, D_MODEL), F32),
        scratch_shapes=[
            pltpu.VMEM((CONV_CH // LANES, tm + HALO, LANES), F32),
            pltpu.VMEM((POOL_CH // LANES, tm + HALO, LANES), F32),
            pltpu.VMEM((tm, CONV_CH), BF16),
            pltpu.VMEM((tm, POOL_CH), BF16),
            pltpu.VMEM((tm, D_MODEL), BF16),
        ],
        compiler_params=pltpu.CompilerParams(
            dimension_semantics=("arbitrary",), vmem_limit_bytes=VMEM_LIMIT),
        name="mixer",
    )(x2, u, u, pu, pu, o, gates, dww, dwb, lng, lnb, wco, wao, pw, ps, wpo, wout, gpost)


def _mlp_kernel(x_ref, p_ref, gpre_ref, w1_ref, w2_ref, gpost_ref, wpp_ref, wpg_ref,
                out_ref, hid_ref):
    x = x_ref[...]
    h = _rmsnorm(x, gpre_ref[...]).astype(BF16)
    fc = 512
    for c in range(0, D_FF, fc):
        a = jnp.dot(h, w1_ref[:, c:c + fc], preferred_element_type=F32)
        a = jnp.maximum(a, 0.0)
        hid_ref[:, c:c + fc] = (a * a).astype(BF16)
    f = jnp.dot(hid_ref[...], w2_ref[...], preferred_element_type=F32)
    x = x + _rmsnorm(f, gpost_ref[...])
    gate = _sigmoid(jnp.dot(x.astype(BF16), wpg_ref[...], preferred_element_type=F32))
    pe = jnp.dot(p_ref[...].astype(BF16), wpp_ref[...], preferred_element_type=F32)
    out_ref[...] = x + gate * pe


def _mlp(x2, p3, gpre, w1, w2, gpost, wpp, wpg, layer):
    n = x2.shape[0]
    tm = TM_MLP
    row = lambda width: pl.BlockSpec((tm, width), lambda i: (i, 0))
    return pl.pallas_call(
        _mlp_kernel,
        grid=(n // tm,),
        in_specs=[
            row(D_MODEL), pl.BlockSpec((None, tm, PLE_DIM), lambda i: (layer, i, 0)),
            _resident((1, D_MODEL), layer),
            _resident((D_MODEL, D_FF), layer), _resident((D_FF, D_MODEL), layer),
            _resident((1, D_MODEL), layer),
            _resident((PLE_DIM, D_MODEL), layer), _resident((D_MODEL, D_MODEL), layer),
        ],
        out_specs=row(D_MODEL),
        out_shape=jax.ShapeDtypeStruct((n, D_MODEL), F32),
        scratch_shapes=[pltpu.VMEM((tm, D_FF), BF16)],
        compiler_params=pltpu.CompilerParams(
            dimension_semantics=("arbitrary",), vmem_limit_bytes=VMEM_LIMIT),
        name="mlp_ple",
    )(x2, p3, gpre, w1, w2, gpost, wpp, wpg)


def kernel(x, p, rel_bias, g_pre_mix, w_in, conv_dw_w, conv_dw_b, conv_ln_g, conv_ln_b, w_conv_out, lam_p, subln_g, w_attn_out, pool_w, pool_scale, w_pool_out, w_out, g_post_mix, g_pre_mlp, w_mlp_in, w_mlp_out, g_post_mlp, w_ple_proj, w_ple_gate):
    batch, seq, d = x.shape
    depth = w_in.shape[0]
    assert d == D_MODEL and seq % TM_MIX == 0 and seq % TQ == 0 and TQ == TK
    n = batch * seq
    x2 = x.reshape(n, d)
    bias_blocks = _near_bias_blocks(rel_bias)
    rows = lambda a: a.reshape(depth, 1, -1)
    bf = lambda a: a.astype(BF16)
    w_in_bf = bf(w_in)
    wvt_bf = bf(jnp.swapaxes(w_in[:, :, V_COL:V_COL + ATTN_W], 1, 2))
    p3 = p.reshape(depth, n, PLE_DIM)
    mixer_params = (conv_dw_w.reshape(depth, CONV_WIDTH, CONV_CH), rows(conv_dw_b),
                    rows(conv_ln_g), rows(conv_ln_b), bf(w_conv_out), bf(w_attn_out),
                    bf(pool_w), rows(pool_scale), bf(w_pool_out), bf(w_out), rows(g_post_mix))
    mlp_params = (rows(g_pre_mlp), bf(w_mlp_in), bf(w_mlp_out), rows(g_post_mlp),
                  bf(w_ple_proj), bf(w_ple_gate))
    g_pre, subln = rows(g_pre_mix), rows(subln_g)
    for layer in range(depth):
        u, q, k, vt, pu, gates = _proj_in(x2, g_pre, w_in_bf, wvt_bf, layer)
        o = _attention(q, k, vt, bias_blocks, lam_p, subln, batch, seq, layer)
        x2 = _mixer(x2, u, pu, o, gates, *mixer_params, seq, layer)
        x2 = _mlp(x2, p3, *mlp_params, layer)
    return x2.reshape(batch, seq, d)
```

```python
import functools
import math

import jax
import jax.numpy as jnp
import numpy as np
from jax import lax
from jax.experimental import pallas as pl
from jax.experimental.pallas import tpu as pltpu

F32 = jnp.float32
BF16 = jnp.bfloat16

D_MODEL = 1024
PLE_DIM = 256
CONV_CH = 512
CONV_WIDTH = 31
ATTN_HEADS = 4
ATTN_HD = 64
ATTN_VD = 2 * ATTN_HD
ATTN_W = ATTN_HEADS * ATTN_VD
POOL_CH = 512
POOL_WINDOWS = (2, 4, 8, 16)
POOL_GC = POOL_CH // len(POOL_WINDOWS)
N_BRANCH = 3
D_FF = 4 * D_MODEL
REL_BUCKETS = 32
REL_MAX_DIST = 128
EPS = 1e-6
MASK_VALUE = -1e30

TM_PROJ = 1024
TM_MIX = 1024
TM_MLP = 1024
TQ = 512
TK = 512
SUB = 128
HEADS_PER_STEP = 2
ONES_ROWS = 16
ACC_ROWS = ATTN_VD + ONES_ROWS
LOG2E = math.log2(math.e)
HALO = 32
CONV_RB = 64
MIX_ROWS = 256
LANES = 128
VMEM_LIMIT = 56 * 1024 * 1024


def _sigmoid(x):
    return 0.5 * jnp.tanh(0.5 * x) + 0.5


def _rmsnorm(x, g):
    return x * lax.rsqrt(jnp.mean(x * x, axis=-1, keepdims=True) + EPS) * g


def _resident(shape, layer):
    index = (layer,) + (0,) * len(shape)
    return pl.BlockSpec((None,) + tuple(shape), lambda *_: index, pipeline_mode=pl.Buffered(1))


V_COL = 2 * CONV_CH + 2 * ATTN_W


def _proj_in_kernel(x_ref, g_ref, w_ref, wvt_ref, u_ref, q_ref, k_ref, vt_ref, pu_ref, gate_ref):
    h = _rmsnorm(x_ref[...], g_ref[...]).astype(BF16)

    def proj(c0, width):
        return jnp.dot(h, w_ref[:, c0:c0 + width], preferred_element_type=F32)

    ca = proj(0, CONV_CH)
    cb = proj(CONV_CH, CONV_CH)
    u_ref[...] = (ca * _sigmoid(cb)).astype(BF16)
    off = 2 * CONV_CH
    q_ref[...] = (proj(off, ATTN_W) * (ATTN_HD ** -0.5 * LOG2E)).astype(BF16)
    k_ref[...] = proj(off + ATTN_W, ATTN_W).astype(BF16)
    vt_ref[...] = lax.dot_general(wvt_ref[...], h, (((1,), (1,)), ((), ())),
                                  preferred_element_type=F32).astype(BF16)
    pu_ref[...] = proj(V_COL + ATTN_W, POOL_CH).astype(BF16)
    off = V_COL + ATTN_W + POOL_CH
    for c in range(0, N_BRANCH * D_MODEL, 512):
        gate_ref[:, c:c + 512] = _sigmoid(proj(off + c, 512)).astype(BF16)


def _proj_in(x2, g, w_bf, wvt_bf, layer):
    n = x2.shape[0]
    d_in = w_bf.shape[-1]
    row = lambda width: pl.BlockSpec((TM_PROJ, width), lambda i: (i, 0))
    widths = (CONV_CH, ATTN_W, ATTN_W, None, POOL_CH, N_BRANCH * D_MODEL)
    out_specs = [row(w) if w else pl.BlockSpec((ATTN_W, TM_PROJ), lambda i: (0, i)) for w in widths]
    out_shape = [jax.ShapeDtypeStruct((n, w) if w else (ATTN_W, n), BF16) for w in widths]
    return pl.pallas_call(
        _proj_in_kernel,
        grid=(n // TM_PROJ,),
        in_specs=[row(D_MODEL), _resident((1, D_MODEL), layer), _resident((D_MODEL, d_in), layer),
                  _resident((ATTN_W, D_MODEL), layer)],
        out_specs=out_specs,
        out_shape=out_shape,
        compiler_params=pltpu.CompilerParams(
            dimension_semantics=("arbitrary",), vmem_limit_bytes=VMEM_LIMIT),
        name="proj_in",
    )(x2, g, w_bf, wvt_bf)


def _rel_bucket_table(n_max):
    n = np.arange(n_max, dtype=np.int32)
    max_exact = REL_BUCKETS // 2
    nf = np.maximum(n, 1).astype(np.float32)
    large = max_exact + (np.log(nf / max_exact) / np.float32(math.log(REL_MAX_DIST / max_exact))
                         * (REL_BUCKETS - max_exact)).astype(np.int32)
    large = np.minimum(large, REL_BUCKETS - 1)
    return np.where(n < max_exact, n, large)


def _near_bias_blocks(rel_bias):
    assert (_rel_bucket_table(8 * TK)[TK + 1:] == REL_BUCKETS - 1).all()
    heads = rel_bias.shape[1]
    dist = np.arange(-(TK - 1), TK + TQ)
    bucket = _rel_bucket_table(TK + TQ)[np.maximum(dist, 0)]
    shifted = (rel_bias.astype(F32) - rel_bias[REL_BUCKETS - 1].astype(F32)) * LOG2E
    line = jnp.where(dist[None, :] >= 0, shifted[bucket].T, MASK_VALUE)
    n_delta = 2 * (TK // SUB) - 1
    win = 2 * SUB - 1
    starts = [d * TK + dp * SUB for d in range(2) for dp in range(n_delta)]
    w = jnp.stack([lax.slice_in_dim(line, s0, s0 + win, axis=1) for s0 in starts], axis=1)
    w = jnp.pad(w, ((0, 0), (0, 0), (0, 1)))
    m = jnp.tile(w, (1, 1, SUB))[:, :, :SUB * win].reshape(heads, 2 * n_delta, SUB, win)
    blocks = m[:, :, :, SUB - 1:2 * SUB - 1]
    return blocks.reshape(heads, 2, n_delta, SUB, SUB)


def _attn_kernel(q_ref, k_ref, vt_ref, bias_ref, lamp_ref, g_ref, o_ref,
                 qb_ref, m_ref, acc_ref, s_ref, p_ref, *, lambda_init):
    nsub = TQ // SUB
    lp = lamp_ref[...]
    lam = (jnp.exp(jnp.sum(lp[0:1] * lp[1:2], axis=-1, keepdims=True))
           - jnp.exp(jnp.sum(lp[2:3] * lp[3:4], axis=-1, keepdims=True)) + lambda_init)

    def q_tile(i, carry):
        _attn_q_tile(i, lam, q_ref, k_ref, vt_ref, bias_ref, g_ref, o_ref,
                     qb_ref, m_ref, acc_ref, s_ref, p_ref, lambda_init)
        return carry

    lax.fori_loop(0, q_ref.shape[0] // TQ, q_tile, 0)


def _attn_q_tile(i, lam, q_ref, k_ref, vt_ref, bias_ref, g_ref, o_ref,
                 qb_ref, m_ref, acc_ref, s_ref, p_ref, lambda_init):
    nsub = TQ // SUB
    chains = [(ch, ch // nsub, ch % nsub) for ch in range(HEADS_PER_STEP * nsub)]
    hcols = lambda hh: slice(hh * ATTN_VD, (hh + 1) * ATTN_VD)
    q0 = pl.multiple_of(i * TQ, TQ)
    lane = lax.broadcasted_iota(jnp.int32, (SUB, ATTN_VD), 1)
    for ch, hh, c in chains:
        qc = q_ref[pl.ds(q0 + c * SUB, SUB), hcols(hh)]
        zero = jnp.zeros_like(qc)
        qb_ref[ch, 0:SUB, :] = jnp.where(lane < ATTN_HD, qc, zero)
        qb_ref[ch, SUB:2 * SUB, :] = jnp.where(lane >= ATTN_HD, qc, zero)
        m_ref[ch] = jnp.full((1, 2 * SUB), MASK_VALUE, F32)
        acc_ref[ch] = jnp.zeros((ACC_ROWS, 2 * SUB), F32)

    def step(ks, n_keys, variant):
        tile_max = []
        for ch, hh, c in chains:
            nk = n_keys[c]
            s = lax.dot_general(k_ref[pl.ds(ks, nk), hcols(hh)], qb_ref[ch], (((1,), (1,)), ((), ())),
                                preferred_element_type=F32)
            if variant is not None:
                b = jnp.concatenate([bias_ref[hh, variant, c - bk + TK // SUB - 1]
                                     for bk in range(nk // SUB)], axis=0)
                s = s + jnp.concatenate([b, b], axis=1)
            s_ref[ch, 0:nk, :] = s
            tile_max.append(jnp.max(s, axis=0, keepdims=True))
        alphas = []
        for ch, hh, c in chains:
            nk = n_keys[c]
            m_old = m_ref[ch]
            m_new = jnp.maximum(m_old, tile_max[ch])
            alpha = jnp.exp2(m_old - m_new)
            p_ref[ch, 0:nk, :] = jnp.exp2(s_ref[ch, 0:nk, :] - m_new).astype(BF16)
            m_ref[ch] = m_new
            alphas.append(alpha)
        for ch, hh, c in chains:
            nk = n_keys[c]
            vte = jnp.concatenate([vt_ref[hcols(hh), pl.ds(ks, nk)], jnp.ones((ONES_ROWS, nk), BF16)],
                                  axis=0)
            pv = jnp.dot(vte, p_ref[ch, 0:nk, :], preferred_element_type=F32)
            acc_ref[ch] = alphas[ch] * acc_ref[ch] + pv

    full = (TK,) * nsub

    def far(j, carry):
        step(pl.multiple_of(j * TK, TK), full, None)
        return carry

    lax.fori_loop(0, jnp.maximum(i - 1, 0), far, 0)

    @pl.when(i >= 1)
    def _():
        step(pl.multiple_of((i - 1) * TK, TK), full, 1)

    step(pl.multiple_of(i * TK, TK), tuple((c + 1) * SUB for c in range(nsub)), 0)

    for ch, hh, c in chains:
        acc = acc_ref[ch, 0:ATTN_VD, :]
        l = acc_ref[ch, ATTN_VD:ATTN_VD + 1, :]
        ot = acc[:, :SUB] / l[:, :SUB] - lam * (acc[:, SUB:] / l[:, SUB:])
        o = _rmsnorm(ot.T, g_ref[...]) * (1.0 - lambda_init)
        o_ref[pl.ds(q0 + c * SUB, SUB), hcols(hh)] = o.astype(BF16)


def _attention(q, k, vt, bias_blocks, lam_p, subln_g, batch, seq, layer):
    n = batch * seq
    nchain = HEADS_PER_STEP * (TQ // SUB)
    hw = HEADS_PER_STEP * ATTN_VD
    lambda_init = 0.8 - 0.6 * math.exp(-0.3 * layer)
    return pl.pallas_call(
        functools.partial(_attn_kernel, lambda_init=lambda_init),
        grid=(batch, ATTN_HEADS // HEADS_PER_STEP),
        in_specs=[
            pl.BlockSpec((seq, hw), lambda b, h: (b, h)),
            pl.BlockSpec((seq, hw), lambda b, h: (b, h)),
            pl.BlockSpec((hw, seq), lambda b, h: (h, b)),
            pl.BlockSpec((HEADS_PER_STEP, 2, 2 * (TK // SUB) - 1, SUB, SUB),
                         lambda b, h: (h, 0, 0, 0, 0)),
            pl.BlockSpec((None, 4, ATTN_HD), lambda b, h: (layer, 0, 0)),
            pl.BlockSpec((None, 1, ATTN_VD), lambda b, h: (layer, 0, 0)),
        ],
        out_specs=pl.BlockSpec((seq, hw), lambda b, h: (b, h)),
        out_shape=jax.ShapeDtypeStruct((n, ATTN_W), BF16),
        scratch_shapes=[
            pltpu.VMEM((nchain, 2 * SUB, ATTN_VD), BF16),
            pltpu.VMEM((nchain, 1, 2 * SUB), F32),
            pltpu.VMEM((nchain, ACC_ROWS, 2 * SUB), F32),
            pltpu.VMEM((nchain, TK, 2 * SUB), F32),
            pltpu.VMEM((nchain, TK, 2 * SUB), BF16),
        ],
        compiler_params=pltpu.CompilerParams(
            dimension_semantics=("arbitrary", "arbitrary"),
            vmem_limit_bytes=VMEM_LIMIT),
        name="diff_attn",
    )(q, k, vt, bias_blocks, lam_p, subln_g)


def _mixer_kernel(x_ref, u_ref, uh_ref, pu_ref, ph_ref, o_ref, gate_ref,
                  dww_ref, dwb_ref, lng_ref, lnb_ref, wco_ref, wao_ref,
                  pw_ref, ps_ref, wpo_ref, wout_ref, gpost_ref,
                  out_ref,
                  ext_ref, pext_ref, cs_ref, pool_ref, merged_ref, *, tiles_per_seq):
    tm = TM_MIX
    t_in_seq = lax.rem(pl.program_id(0), tiles_per_seq)
    keep = t_in_seq > 0
    n_slab = CONV_CH // LANES

    def fill(dst_ref, tile_ref, halo_ref):
        for s in range(n_slab):
            cols = slice(s * LANES, (s + 1) * LANES)
            dst_ref[s, 0:HALO, :] = jnp.where(keep, halo_ref[:, cols].astype(F32), 0.0)
            dst_ref[s, HALO:HALO + tm, :] = tile_ref[:, cols].astype(F32)

    fill(ext_ref, u_ref, uh_ref)
    first_off = HALO - (CONV_WIDTH - 1)

    def conv_rows(base):
        conv = []
        for s in range(n_slab):
            cols = slice(s * LANES, (s + 1) * LANES)
            acc = jnp.zeros((CONV_RB, LANES), F32)
            for j in range(CONV_WIDTH):
                acc = acc + ext_ref[s, pl.ds(base + first_off + j, CONV_RB), :] * dww_ref[j:j + 1, cols]
            conv.append(acc + dwb_ref[:, cols])
        mu = sum(jnp.sum(a, axis=-1, keepdims=True) for a in conv) * (1.0 / CONV_CH)
        cen = [a - mu for a in conv]
        var = sum(jnp.sum(a * a, axis=-1, keepdims=True) for a in cen) * (1.0 / CONV_CH)
        inv = lax.rsqrt(var + EPS)
        for s in range(n_slab):
            cols = slice(s * LANES, (s + 1) * LANES)
            y = cen[s] * inv * lng_ref[:, cols] + lnb_ref[:, cols]
            cs_ref[pl.ds(base, CONV_RB), cols] = (y * _sigmoid(y)).astype(BF16)

    fill(pext_ref, pu_ref, ph_ref)
    pos = t_in_seq * tm + lax.broadcasted_iota(jnp.int32, (tm, POOL_GC), 0)
    for g, w in enumerate(POOL_WINDOWS):
        cols = slice(g * POOL_GC, (g + 1) * POOL_GC)
        e = pext_ref[g, HALO:HALO + tm, :]
        tot = e
        for j in range(1, w):
            tot = tot + pext_ref[g, HALO - j:HALO - j + tm, :]
        cnt = jnp.minimum(pos + 1, w).astype(F32)
        pooled = tot / cnt - e
        yg = jnp.dot(pooled.astype(BF16), pw_ref[g], preferred_element_type=F32)
        pool_ref[:, cols] = (yg * ps_ref[:, cols]).astype(BF16)

    nc = 256
    for r0 in range(0, tm, MIX_ROWS):
        rows = slice(r0, r0 + MIX_ROWS)
        for base in range(r0, r0 + MIX_ROWS, CONV_RB):
            conv_rows(base)
        for c in range(0, D_MODEL, nc):
            cols = slice(c, c + nc)
            y_conv = jnp.dot(cs_ref[rows, :], wco_ref[:, cols], preferred_element_type=F32)
            y_attn = jnp.dot(o_ref[rows, :], wao_ref[:, cols], preferred_element_type=F32)
            y_pool = jnp.dot(pool_ref[rows, :], wpo_ref[:, cols], preferred_element_type=F32)
            g0 = gate_ref[rows, c:c + nc].astype(F32)
            g1 = gate_ref[rows, D_MODEL + c:D_MODEL + c + nc].astype(F32)
            g2 = gate_ref[rows, 2 * D_MODEL + c:2 * D_MODEL + c + nc].astype(F32)
            merged_ref[rows, cols] = (g0 * y_conv + g1 * y_attn + g2 * y_pool).astype(BF16)
        mix = jnp.dot(merged_ref[rows, :], wout_ref[...], preferred_element_type=F32)
        out_ref[rows, :] = x_ref[rows, :] + _rmsnorm(mix, gpost_ref[...])


def _mixer(x2, u, pu, o, gates, dww, dwb, lng, lnb, wco, wao, pw, ps, wpo, wout, gpost, seq, layer):
    n = x2.shape[0]
    tm = TM_MIX
    tiles_per_seq = seq // tm
    row = lambda width: pl.BlockSpec((tm, width), lambda i: (i, 0))
    halo = lambda width: pl.BlockSpec(
        (HALO, width), lambda i: (jnp.maximum(i * (tm // HALO) - 1, 0), 0))
    return pl.pallas_call(
        functools.partial(_mixer_kernel, tiles_per_seq=tiles_per_seq),
        grid=(n // tm,),
        in_specs=[
            row(D_MODEL), row(CONV_CH), halo(CONV_CH), row(POOL_CH), halo(POOL_CH),
            row(ATTN_W), row(N_BRANCH * D_MODEL),
            _resident((CONV_WIDTH, CONV_CH), layer), _resident((1, CONV_CH), layer),
            _resident((1, CONV_CH), layer), _resident((1, CONV_CH), layer),
            _resident((CONV_CH, D_MODEL), layer), _resident((ATTN_W, D_MODEL), layer),
            _resident((len(POOL_WINDOWS), POOL_GC, POOL_GC), layer), _resident((1, POOL_CH), layer),
            _resident((POOL_CH, D_MODEL), layer), _resident((D_MODEL, D_MODEL), layer),
            _resident((1, D_MODEL), layer),
        ],
        out_specs=row(D_MODEL),
        out_shape=jax.ShapeDtypeStruct((n, D_MODEL), F32),
        scratch_shapes=[
            pltpu.VMEM((CONV_CH // LANES, tm + HALO, LANES), F32),
            pltpu.VMEM((POOL_CH // LANES, tm + HALO, LANES), F32),
            pltpu.VMEM((tm, CONV_CH), BF16),
            pltpu.VMEM((tm, POOL_CH), BF16),
            pltpu.VMEM((tm, D_MODEL), BF16),
        ],
        compiler_params=pltpu.CompilerParams(
            dimension_semantics=("arbitrary",), vmem_limit_bytes=VMEM_LIMIT),
        name="mixer",
    )(x2, u, u, pu, pu, o, gates, dww, dwb, lng, lnb, wco, wao, pw, ps, wpo, wout, gpost)


def _mlp_kernel(x_ref, p_ref, gpre_ref, w1_ref, w2_ref, gpost_ref, wpp_ref, wpg_ref,
                out_ref, hid_ref):
    x = x_ref[...]
    h = _rmsnorm(x, gpre_ref[...]).astype(BF16)
    fc = 512
    for c in range(0, D_FF, fc):
        a = jnp.dot(h, w1_ref[:, c:c + fc], preferred_element_type=F32)
        a = jnp.maximum(a, 0.0)
        hid_ref[:, c:c + fc] = (a * a).astype(BF16)
    f = jnp.dot(hid_ref[...], w2_ref[...], preferred_element_type=F32)
    x = x + _rmsnorm(f, gpost_ref[...])
    gate = _sigmoid(jnp.dot(x.astype(BF16), wpg_ref[...], preferred_element_type=F32))
    pe = jnp.dot(p_ref[...].astype(BF16), wpp_ref[...], preferred_element_type=F32)
    out_ref[...] = x + gate * pe


def _mlp(x2, p3, gpre, w1, w2, gpost, wpp, wpg, layer):
    n = x2.shape[0]
    tm = TM_MLP
    row = lambda width: pl.BlockSpec((tm, width), lambda i: (i, 0))
    return pl.pallas_call(
        _mlp_kernel,
        grid=(n // tm,),
        in_specs=[
            row(D_MODEL), pl.BlockSpec((None, tm, PLE_DIM), lambda i: (layer, i, 0)),
            _resident((1, D_MODEL), layer),
            _resident((D_MODEL, D_FF), layer), _resident((D_FF, D_MODEL), layer),
            _resident((1, D_MODEL), layer),
            _resident((PLE_DIM, D_MODEL), layer), _resident((D_MODEL, D_MODEL), layer),
        ],
        out_specs=row(D_MODEL),
        out_shape=jax.ShapeDtypeStruct((n, D_MODEL), F32),
        scratch_shapes=[pltpu.VMEM((tm, D_FF), BF16)],
        compiler_params=pltpu.CompilerParams(
            dimension_semantics=("arbitrary",), vmem_limit_bytes=VMEM_LIMIT),
        name="mlp_ple",
    )(x2, p3, gpre, w1, w2, gpost, wpp, wpg)


def kernel(x, p, rel_bias, g_pre_mix, w_in, conv_dw_w, conv_dw_b, conv_ln_g, conv_ln_b, w_conv_out, lam_p, subln_g, w_attn_out, pool_w, pool_scale, w_pool_out, w_out, g_post_mix, g_pre_mlp, w_mlp_in, w_mlp_out, g_post_mlp, w_ple_proj, w_ple_gate):
    batch, seq, d = x.shape
    depth = w_in.shape[0]
    assert d == D_MODEL and seq % TM_MIX == 0 and seq % TQ == 0 and TQ == TK
    n = batch * seq
    x2 = x.reshape(n, d)
    bias_blocks = _near_bias_blocks(rel_bias)
    rows = lambda a: a.reshape(depth, 1, -1)
    bf = lambda a: a.astype(BF16)
    w_in_bf = bf(w_in)
    wvt_bf = bf(jnp.swapaxes(lax.optimization_barrier(w_in[:, :, V_COL:V_COL + ATTN_W]), 1, 2))
    p3 = p.reshape(depth, n, PLE_DIM)
    mixer_params = (conv_dw_w.reshape(depth, CONV_WIDTH, CONV_CH), rows(conv_dw_b),
                    rows(conv_ln_g), rows(conv_ln_b), bf(w_conv_out), bf(w_attn_out),
                    bf(pool_w), rows(pool_scale), bf(w_pool_out), bf(w_out), rows(g_post_mix))
    mlp_params = (rows(g_pre_mlp), bf(w_mlp_in), bf(w_mlp_out), rows(g_post_mlp),
                  bf(w_ple_proj), bf(w_ple_gate))
    g_pre, subln = rows(g_pre_mix), rows(subln_g)
    for layer in range(depth):
        u, q, k, vt, pu, gates = _proj_in(x2, g_pre, w_in_bf, wvt_bf, layer)
        o = _attention(q, k, vt, bias_blocks, lam_p, subln, batch, seq, layer)
        x2 = _mixer(x2, u, pu, o, gates, *mixer_params, seq, layer)
        x2 = _mlp(x2, p3, *mlp_params, layer)
    return x2.reshape(batch, seq, d)
```

```python
import functools
import math

import jax
import jax.numpy as jnp
import numpy as np
from jax import lax
from jax.experimental import pallas as pl
from jax.experimental.pallas import tpu as pltpu

F32 = jnp.float32
BF16 = jnp.bfloat16

D_MODEL = 1024
PLE_DIM = 256
CONV_CH = 512
CONV_WIDTH = 31
ATTN_HEADS = 4
ATTN_HD = 64
ATTN_VD = 2 * ATTN_HD
ATTN_W = ATTN_HEADS * ATTN_VD
POOL_CH = 512
POOL_WINDOWS = (2, 4, 8, 16)
POOL_GC = POOL_CH // len(POOL_WINDOWS)
N_BRANCH = 3
D_FF = 4 * D_MODEL
REL_BUCKETS = 32
REL_MAX_DIST = 128
EPS = 1e-6
MASK_VALUE = -1e30

TM_PROJ = 1024
TM_MIX = 1024
TM_MLP = 1024
TQ = 1024
TK = 1024
SUB = 128
HEADS_PER_STEP = 1
ONES_ROWS = 16
ACC_ROWS = ATTN_VD + ONES_ROWS
LOG2E = math.log2(math.e)
HALO = 32
CONV_RB = 64
MIX_ROWS = 256
LANES = 128
VMEM_LIMIT = 56 * 1024 * 1024


def _sigmoid(x):
    return 0.5 * jnp.tanh(0.5 * x) + 0.5


def _rmsnorm(x, g):
    return x * lax.rsqrt(jnp.mean(x * x, axis=-1, keepdims=True) + EPS) * g


def _resident(shape, layer):
    index = (layer,) + (0,) * len(shape)
    return pl.BlockSpec((None,) + tuple(shape), lambda *_: index, pipeline_mode=pl.Buffered(1))


V_COL = 2 * CONV_CH + 2 * ATTN_W


def _proj_in_kernel(x_ref, g_ref, w_ref, wvt_ref, u_ref, q_ref, k_ref, vt_ref, pu_ref, gate_ref):
    h = _rmsnorm(x_ref[...], g_ref[...]).astype(BF16)

    def proj(c0, width):
        return jnp.dot(h, w_ref[:, c0:c0 + width], preferred_element_type=F32)

    ca = proj(0, CONV_CH)
    cb = proj(CONV_CH, CONV_CH)
    u_ref[...] = (ca * _sigmoid(cb)).astype(BF16)
    off = 2 * CONV_CH
    q_ref[...] = (proj(off, ATTN_W) * (ATTN_HD ** -0.5 * LOG2E)).astype(BF16)
    k_ref[...] = proj(off + ATTN_W, ATTN_W).astype(BF16)
    vt_ref[...] = lax.dot_general(wvt_ref[...], h, (((1,), (1,)), ((), ())),
                                  preferred_element_type=F32).astype(BF16)
    pu_ref[...] = proj(V_COL + ATTN_W, POOL_CH).astype(BF16)
    off = V_COL + ATTN_W + POOL_CH
    for c in range(0, N_BRANCH * D_MODEL, 512):
        gate_ref[:, c:c + 512] = _sigmoid(proj(off + c, 512)).astype(BF16)


def _proj_in(x2, g, w_bf, wvt_bf, layer):
    n = x2.shape[0]
    d_in = w_bf.shape[-1]
    row = lambda width: pl.BlockSpec((TM_PROJ, width), lambda i: (i, 0))
    widths = (CONV_CH, ATTN_W, ATTN_W, None, POOL_CH, N_BRANCH * D_MODEL)
    out_specs = [row(w) if w else pl.BlockSpec((ATTN_W, TM_PROJ), lambda i: (0, i)) for w in widths]
    out_shape = [jax.ShapeDtypeStruct((n, w) if w else (ATTN_W, n), BF16) for w in widths]
    return pl.pallas_call(
        _proj_in_kernel,
        grid=(n // TM_PROJ,),
        in_specs=[row(D_MODEL), _resident((1, D_MODEL), layer), _resident((D_MODEL, d_in), layer),
                  _resident((ATTN_W, D_MODEL), layer)],
        out_specs=out_specs,
        out_shape=out_shape,
        compiler_params=pltpu.CompilerParams(
            dimension_semantics=("arbitrary",), vmem_limit_bytes=VMEM_LIMIT),
        name="proj_in",
    )(x2, g, w_bf, wvt_bf)


def _rel_bucket_table(n_max):
    n = np.arange(n_max, dtype=np.int32)
    max_exact = REL_BUCKETS // 2
    nf = np.maximum(n, 1).astype(np.float32)
    large = max_exact + (np.log(nf / max_exact) / np.float32(math.log(REL_MAX_DIST / max_exact))
                         * (REL_BUCKETS - max_exact)).astype(np.int32)
    large = np.minimum(large, REL_BUCKETS - 1)
    return np.where(n < max_exact, n, large)


def _near_bias_blocks(rel_bias):
    assert (_rel_bucket_table(8 * TK)[TK + 1:] == REL_BUCKETS - 1).all()
    heads = rel_bias.shape[1]
    dist = np.arange(-(TK - 1), TK + TQ)
    bucket = _rel_bucket_table(TK + TQ)[np.maximum(dist, 0)]
    shifted = (rel_bias.astype(F32) - rel_bias[REL_BUCKETS - 1].astype(F32)) * LOG2E
    line = jnp.where(dist[None, :] >= 0, shifted[bucket].T, MASK_VALUE)
    n_delta = 2 * (TK // SUB) - 1
    win = 2 * SUB - 1
    starts = [d * TK + dp * SUB for d in range(2) for dp in range(n_delta)]
    w = jnp.stack([lax.slice_in_dim(line, s0, s0 + win, axis=1) for s0 in starts], axis=1)
    w = jnp.pad(w, ((0, 0), (0, 0), (0, 1)))
    m = jnp.tile(w, (1, 1, SUB))[:, :, :SUB * win].reshape(heads, 2 * n_delta, SUB, win)
    blocks = m[:, :, :, SUB - 1:2 * SUB - 1]
    return blocks.reshape(heads, 2, n_delta, SUB, SUB)


def _bias_block_nonzero(variant, delta):
    min_dist = variant * TK + delta * SUB - (SUB - 1)
    far_from = int(np.argmax(_rel_bucket_table(4 * TK) == REL_BUCKETS - 1))
    return min_dist < far_from


def _attn_kernel(q_ref, k_ref, vt_ref, bias_ref, lamp_ref, g_ref, o_ref,
                 qb_ref, m_ref, acc_ref, s_ref, p_ref, *, lambda_init):
    nsub = TQ // SUB
    lp = lamp_ref[...]
    lam = (jnp.exp(jnp.sum(lp[0:1] * lp[1:2], axis=-1, keepdims=True))
           - jnp.exp(jnp.sum(lp[2:3] * lp[3:4], axis=-1, keepdims=True)) + lambda_init)

    def q_tile(i, carry):
        _attn_q_tile(i, lam, q_ref, k_ref, vt_ref, bias_ref, g_ref, o_ref,
                     qb_ref, m_ref, acc_ref, s_ref, p_ref, lambda_init)
        return carry

    lax.fori_loop(0, q_ref.shape[0] // TQ, q_tile, 0)


def _attn_q_tile(i, lam, q_ref, k_ref, vt_ref, bias_ref, g_ref, o_ref,
                 qb_ref, m_ref, acc_ref, s_ref, p_ref, lambda_init):
    nsub = TQ // SUB
    chains = [(ch, ch // nsub, ch % nsub) for ch in range(HEADS_PER_STEP * nsub)]
    hcols = lambda hh: slice(hh * ATTN_VD, (hh + 1) * ATTN_VD)
    q0 = pl.multiple_of(i * TQ, TQ)
    lane = lax.broadcasted_iota(jnp.int32, (SUB, ATTN_VD), 1)
    for ch, hh, c in chains:
        qc = q_ref[pl.ds(q0 + c * SUB, SUB), hcols(hh)]
        zero = jnp.zeros_like(qc)
        qb_ref[ch, 0:SUB, :] = jnp.where(lane < ATTN_HD, qc, zero)
        qb_ref[ch, SUB:2 * SUB, :] = jnp.where(lane >= ATTN_HD, qc, zero)
        m_ref[ch] = jnp.full((1, 2 * SUB), MASK_VALUE, F32)
        acc_ref[ch] = jnp.zeros((ACC_ROWS, 2 * SUB), F32)

    def step(ks, n_keys, variant):
        tile_max = []
        for ch, hh, c in chains:
            nk = n_keys[c]
            s = lax.dot_general(k_ref[pl.ds(ks, nk), hcols(hh)], qb_ref[ch], (((1,), (1,)), ((), ())),
                                preferred_element_type=F32)
            if variant is not None:
                parts = []
                for bk in range(nk // SUB):
                    blk = s[bk * SUB:(bk + 1) * SUB, :]
                    if _bias_block_nonzero(variant, c - bk):
                        b = bias_ref[hh, variant, c - bk + TK // SUB - 1]
                        blk = blk + jnp.concatenate([b, b], axis=1)
                    parts.append(blk)
                s = jnp.concatenate(parts, axis=0)
            s_ref[ch, 0:nk, :] = s
            tile_max.append(jnp.max(s, axis=0, keepdims=True))
        alphas = []
        for ch, hh, c in chains:
            nk = n_keys[c]
            m_old = m_ref[ch]
            m_new = jnp.maximum(m_old, tile_max[ch])
            alpha = jnp.exp2(m_old - m_new)
            p_ref[ch, 0:nk, :] = jnp.exp2(s_ref[ch, 0:nk, :] - m_new).astype(BF16)
            m_ref[ch] = m_new
            alphas.append(alpha)
        for ch, hh, c in chains:
            nk = n_keys[c]
            vte = jnp.concatenate([vt_ref[hcols(hh), pl.ds(ks, nk)], jnp.ones((ONES_ROWS, nk), BF16)],
                                  axis=0)
            pv = jnp.dot(vte, p_ref[ch, 0:nk, :], preferred_element_type=F32)
            acc_ref[ch] = alphas[ch] * acc_ref[ch] + pv

    full = (TK,) * nsub

    def far(j, carry):
        step(pl.multiple_of(j * TK, TK), full, None)
        return carry

    lax.fori_loop(0, jnp.maximum(i - 1, 0), far, 0)

    @pl.when(i >= 1)
    def _():
        step(pl.multiple_of((i - 1) * TK, TK), full, 1)

    step(pl.multiple_of(i * TK, TK), tuple((c + 1) * SUB for c in range(nsub)), 0)

    for ch, hh, c in chains:
        acc = acc_ref[ch, 0:ATTN_VD, :]
        l = acc_ref[ch, ATTN_VD:ATTN_VD + 1, :]
        ot = acc[:, :SUB] / l[:, :SUB] - lam * (acc[:, SUB:] / l[:, SUB:])
        o = _rmsnorm(ot.T, g_ref[...]) * (1.0 - lambda_init)
        o_ref[pl.ds(q0 + c * SUB, SUB), hcols(hh)] = o.astype(BF16)


def _attention(q, k, vt, bias_blocks, lam_p, subln_g, batch, seq, layer):
    n = batch * seq
    nchain = HEADS_PER_STEP * (TQ // SUB)
    hw = HEADS_PER_STEP * ATTN_VD
    lambda_init = 0.8 - 0.6 * math.exp(-0.3 * layer)
    return pl.pallas_call(
        functools.partial(_attn_kernel, lambda_init=lambda_init),
        grid=(batch, ATTN_HEADS // HEADS_PER_STEP),
        in_specs=[
            pl.BlockSpec((seq, hw), lambda b, h: (b, h)),
            pl.BlockSpec((seq, hw), lambda b, h: (b, h)),
            pl.BlockSpec((hw, seq), lambda b, h: (h, b)),
            pl.BlockSpec((HEADS_PER_STEP, 2, 2 * (TK // SUB) - 1, SUB, SUB),
                         lambda b, h: (h, 0, 0, 0, 0)),
            pl.BlockSpec((None, 4, ATTN_HD), lambda b, h: (layer, 0, 0)),
            pl.BlockSpec((None, 1, ATTN_VD), lambda b, h: (layer, 0, 0)),
        ],
        out_specs=pl.BlockSpec((seq, hw), lambda b, h: (b, h)),
        out_shape=jax.ShapeDtypeStruct((n, ATTN_W), BF16),
        scratch_shapes=[
            pltpu.VMEM((nchain, 2 * SUB, ATTN_VD), BF16),
            pltpu.VMEM((nchain, 1, 2 * SUB), F32),
            pltpu.VMEM((nchain, ACC_ROWS, 2 * SUB), F32),
            pltpu.VMEM((nchain, TK, 2 * SUB), F32),
            pltpu.VMEM((nchain, TK, 2 * SUB), BF16),
        ],
        compiler_params=pltpu.CompilerParams(
            dimension_semantics=("arbitrary", "arbitrary"),
            vmem_limit_bytes=VMEM_LIMIT),
        name="diff_attn",
    )(q, k, vt, bias_blocks, lam_p, subln_g)


def _mixer_kernel(x_ref, u_ref, uh_ref, pu_ref, ph_ref, o_ref, gate_ref,
                  dww_ref, dwb_ref, lng_ref, lnb_ref, wco_ref, wao_ref,
                  pw_ref, ps_ref, wpo_ref, wout_ref, gpost_ref,
                  out_ref,
                  ext_ref, pext_ref, cs_ref, pool_ref, merged_ref, *, tiles_per_seq):
    tm = TM_MIX
    t_in_seq = lax.rem(pl.program_id(0), tiles_per_seq)
    keep = t_in_seq > 0
    n_slab = CONV_CH // LANES

    def fill(dst_ref, tile_ref, halo_ref):
        for s in range(n_slab):
            cols = slice(s * LANES, (s + 1) * LANES)
            dst_ref[s, 0:HALO, :] = jnp.where(keep, halo_ref[:, cols].astype(F32), 0.0)
            dst_ref[s, HALO:HALO + tm, :] = tile_ref[:, cols].astype(F32)

    fill(ext_ref, u_ref, uh_ref)
    first_off = HALO - (CONV_WIDTH - 1)

    def conv_rows(base):
        conv = []
        for s in range(n_slab):
            cols = slice(s * LANES, (s + 1) * LANES)
            acc = jnp.zeros((CONV_RB, LANES), F32)
            for j in range(CONV_WIDTH):
                acc = acc + ext_ref[s, pl.ds(base + first_off + j, CONV_RB), :] * dww_ref[j:j + 1, cols]
            conv.append(acc + dwb_ref[:, cols])
        mu = sum(jnp.sum(a, axis=-1, keepdims=True) for a in conv) * (1.0 / CONV_CH)
        cen = [a - mu for a in conv]
        var = sum(jnp.sum(a * a, axis=-1, keepdims=True) for a in cen) * (1.0 / CONV_CH)
        inv = lax.rsqrt(var + EPS)
        for s in range(n_slab):
            cols = slice(s * LANES, (s + 1) * LANES)
            y = cen[s] * inv * lng_ref[:, cols] + lnb_ref[:, cols]
            cs_ref[pl.ds(base, CONV_RB), cols] = (y * _sigmoid(y)).astype(BF16)

    fill(pext_ref, pu_ref, ph_ref)
    pos = t_in_seq * tm + lax.broadcasted_iota(jnp.int32, (tm, POOL_GC), 0)
    for g, w in enumerate(POOL_WINDOWS):
        cols = slice(g * POOL_GC, (g + 1) * POOL_GC)
        e = pext_ref[g, HALO:HALO + tm, :]
        tot = e
        for j in range(1, w):
            tot = tot + pext_ref[g, HALO - j:HALO - j + tm, :]
        cnt = jnp.minimum(pos + 1, w).astype(F32)
        pooled = tot / cnt - e
        yg = jnp.dot(pooled.astype(BF16), pw_ref[g], preferred_element_type=F32)
        pool_ref[:, cols] = (yg * ps_ref[:, cols]).astype(BF16)

    nc = 256
    for r0 in range(0, tm, MIX_ROWS):
        rows = slice(r0, r0 + MIX_ROWS)
        for base in range(r0, r0 + MIX_ROWS, CONV_RB):
            conv_rows(base)
        for c in range(0, D_MODEL, nc):
            cols = slice(c, c + nc)
            y_conv = jnp.dot(cs_ref[rows, :], wco_ref[:, cols], preferred_element_type=F32)
            y_attn = jnp.dot(o_ref[rows, :], wao_ref[:, cols], preferred_element_type=F32)
            y_pool = jnp.dot(pool_ref[rows, :], wpo_ref[:, cols], preferred_element_type=F32)
            g0 = gate_ref[rows, c:c + nc].astype(F32)
            g1 = gate_ref[rows, D_MODEL + c:D_MODEL + c + nc].astype(F32)
            g2 = gate_ref[rows, 2 * D_MODEL + c:2 * D_MODEL + c + nc].astype(F32)
            merged_ref[rows, cols] = (g0 * y_conv + g1 * y_attn + g2 * y_pool).astype(BF16)
        mix = jnp.dot(merged_ref[rows, :], wout_ref[...], preferred_element_type=F32)
        out_ref[rows, :] = x_ref[rows, :] + _rmsnorm(mix, gpost_ref[...])


def _mixer(x2, u, pu, o, gates, dww, dwb, lng, lnb, wco, wao, pw, ps, wpo, wout, gpost, seq, layer):
    n = x2.shape[0]
    tm = TM_MIX
    tiles_per_seq = seq // tm
    row = lambda width: pl.BlockSpec((tm, width), lambda i: (i, 0))
    halo = lambda width: pl.BlockSpec(
        (HALO, width), lambda i: (jnp.maximum(i * (tm // HALO) - 1, 0), 0))
    return pl.pallas_call(
        functools.partial(_mixer_kernel, tiles_per_seq=tiles_per_seq),
        grid=(n // tm,),
        in_specs=[
            row(D_MODEL), row(CONV_CH), halo(CONV_CH), row(POOL_CH), halo(POOL_CH),
            row(ATTN_W), row(N_BRANCH * D_MODEL),
            _resident((CONV_WIDTH, CONV_CH), layer), _resident((1, CONV_CH), layer),
            _resident((1, CONV_CH), layer), _resident((1, CONV_CH), layer),
            _resident((CONV_CH, D_MODEL), layer), _resident((ATTN_W, D_MODEL), layer),
            _resident((len(POOL_WINDOWS), POOL_GC, POOL_GC), layer), _resident((1, POOL_CH), layer),
            _resident((POOL_CH, D_MODEL), layer), _resident((D_MODEL, D_MODEL), layer),
            _resident((1, D_MODEL), layer),
        ],
        out_specs=row(D_MODEL),
        out_shape=jax.ShapeDtypeStruct((n, D_MODEL), F32),
        scratch_shapes=[
            pltpu.VMEM((CONV_CH // LANES, tm + HALO, LANES), F32),
            pltpu.VMEM((POOL_CH // LANES, tm + HALO, LANES), F32),
            pltpu.VMEM((tm, CONV_CH), BF16),
            pltpu.VMEM((tm, POOL_CH), BF16),
            pltpu.VMEM((tm, D_MODEL), BF16),
        ],
        compiler_params=pltpu.CompilerParams(
            dimension_semantics=("arbitrary",), vmem_limit_bytes=VMEM_LIMIT),
        name="mixer",
    )(x2, u, u, pu, pu, o, gates, dww, dwb, lng, lnb, wco, wao, pw, ps, wpo, wout, gpost)


def _mlp_kernel(x_ref, p_ref, gpre_ref, w1_ref, w2_ref, gpost_ref, wpp_ref, wpg_ref,
                out_ref, hid_ref):
    x = x_ref[...]
    h = _rmsnorm(x, gpre_ref[...]).astype(BF16)
    fc = 512
    for c in range(0, D_FF, fc):
        a = jnp.dot(h, w1_ref[:, c:c + fc], preferred_element_type=F32)
        a = jnp.maximum(a, 0.0)
        hid_ref[:, c:c + fc] = (a * a).astype(BF16)
    f = jnp.dot(hid_ref[...], w2_ref[...], preferred_element_type=F32)
    x = x + _rmsnorm(f, gpost_ref[...])
    gate = _sigmoid(jnp.dot(x.astype(BF16), wpg_ref[...], preferred_element_type=F32))
    pe = jnp.dot(p_ref[...].astype(BF16), wpp_ref[...], preferred_element_type=F32)
    out_ref[...] = x + gate * pe


def _mlp(x2, p3, gpre, w1, w2, gpost, wpp, wpg, layer):
    n = x2.shape[0]
    tm = TM_MLP
    row = lambda width: pl.BlockSpec((tm, width), lambda i: (i, 0))
    return pl.pallas_call(
        _mlp_kernel,
        grid=(n // tm,),
        in_specs=[
            row(D_MODEL), pl.BlockSpec((None, tm, PLE_DIM), lambda i: (layer, i, 0)),
            _resident((1, D_MODEL), layer),
            _resident((D_MODEL, D_FF), layer), _resident((D_FF, D_MODEL), layer),
            _resident((1, D_MODEL), layer),
            _resident((PLE_DIM, D_MODEL), layer), _resident((D_MODEL, D_MODEL), layer),
        ],
        out_specs=row(D_MODEL),
        out_shape=jax.ShapeDtypeStruct((n, D_MODEL), F32),
        scratch_shapes=[pltpu.VMEM((tm, D_FF), BF16)],
        compiler_params=pltpu.CompilerParams(
            dimension_semantics=("arbitrary",), vmem_limit_bytes=VMEM_LIMIT),
        name="mlp_ple",
    )(x2, p3, gpre, w1, w2, gpost, wpp, wpg)


def kernel(x, p, rel_bias, g_pre_mix, w_in, conv_dw_w, conv_dw_b, conv_ln_g, conv_ln_b, w_conv_out, lam_p, subln_g, w_attn_out, pool_w, pool_scale, w_pool_out, w_out, g_post_mix, g_pre_mlp, w_mlp_in, w_mlp_out, g_post_mlp, w_ple_proj, w_ple_gate):
    batch, seq, d = x.shape
    depth = w_in.shape[0]
    assert d == D_MODEL and seq % TM_MIX == 0 and seq % TQ == 0 and TQ == TK
    n = batch * seq
    x2 = x.reshape(n, d)
    bias_blocks = _near_bias_blocks(rel_bias)
    rows = lambda a: a.reshape(depth, 1, -1)
    bf = lambda a: a.astype(BF16)
    w_in_bf = bf(w_in)
    wvt_bf = bf(jnp.swapaxes(lax.optimization_barrier(w_in[:, :, V_COL:V_COL + ATTN_W]), 1, 2))
    p3 = p.reshape(depth, n, PLE_DIM)
    mixer_params = (conv_dw_w.reshape(depth, CONV_WIDTH, CONV_CH), rows(conv_dw_b),
                    rows(conv_ln_g), rows(conv_ln_b), bf(w_conv_out), bf(w_attn_out),
                    bf(pool_w), rows(pool_scale), bf(w_pool_out), bf(w_out), rows(g_post_mix))
    mlp_params = (rows(g_pre_mlp), bf(w_mlp_in), bf(w_mlp_out), rows(g_post_mlp),
                  bf(w_ple_proj), bf(w_ple_gate))
    g_pre, subln = rows(g_pre_mix), rows(subln_g)
    for layer in range(depth):
        u, q, k, vt, pu, gates = _proj_in(x2, g_pre, w_in_bf, wvt_bf, layer)
        o = _attention(q, k, vt, bias_blocks, lam_p, subln, batch, seq, layer)
        x2 = _mixer(x2, u, pu, o, gates, *mixer_params, seq, layer)
        x2 = _mlp(x2, p3, *mlp_params, layer)
    return x2.reshape(batch, seq, d)
```

```python
import functools
import math

import jax
import jax.numpy as jnp
import numpy as np
from jax import lax
from jax.experimental import pallas as pl
from jax.experimental.pallas import tpu as pltpu

F32 = jnp.float32
BF16 = jnp.bfloat16

D_MODEL = 1024
PLE_DIM = 256
CONV_CH = 512
CONV_WIDTH = 31
ATTN_HEADS = 4
ATTN_HD = 64
ATTN_VD = 2 * ATTN_HD
ATTN_W = ATTN_HEADS * ATTN_VD
POOL_CH = 512
POOL_WINDOWS = (2, 4, 8, 16)
POOL_GC = POOL_CH // len(POOL_WINDOWS)
N_BRANCH = 3
D_FF = 4 * D_MODEL
REL_BUCKETS = 32
REL_MAX_DIST = 128
EPS = 1e-6
MASK_VALUE = -1e30

TM_PROJ = 1024
PROJ_ROWS = 256
TM_MIX = 1024
TM_MLP = 1024
TQ = 1024
TK = 1024
SUB = 128
HEADS_PER_STEP = 2
ONES_ROWS = 16
ACC_ROWS = ATTN_VD + ONES_ROWS
LOG2E = math.log2(math.e)
HALO = 32
CONV_RB = 64
MIX_ROWS = 256
LANES = 128
VMEM_LIMIT = 60000 * 1024


def _sigmoid(x):
    return 0.5 * jnp.tanh(0.5 * x) + 0.5


def _rmsnorm(x, g):
    return x * lax.rsqrt(jnp.mean(x * x, axis=-1, keepdims=True) + EPS) * g


def _resident(shape, layer):
    index = (layer,) + (0,) * len(shape)
    return pl.BlockSpec((None,) + tuple(shape), lambda *_: index, pipeline_mode=pl.Buffered(1))


V_COL = 2 * CONV_CH + 2 * ATTN_W


def _proj_in_kernel(x_ref, g_ref, w_ref, wvt_ref, u_ref, q_ref, k_ref, vt_ref, pu_ref, gate_ref):
    for r0 in range(0, TM_PROJ, PROJ_ROWS):
        rows = slice(r0, r0 + PROJ_ROWS)
        h = _rmsnorm(x_ref[rows, :], g_ref[...]).astype(BF16)

        def proj(c0, width):
            return jnp.dot(h, w_ref[:, c0:c0 + width], preferred_element_type=F32)

        ca = proj(0, CONV_CH)
        cb = proj(CONV_CH, CONV_CH)
        u_ref[rows, :] = (ca * _sigmoid(cb)).astype(BF16)
        off = 2 * CONV_CH
        q_ref[rows, :] = (proj(off, ATTN_W) * (ATTN_HD ** -0.5 * LOG2E)).astype(BF16)
        k_ref[rows, :] = proj(off + ATTN_W, ATTN_W).astype(BF16)
        vt_ref[:, rows] = lax.dot_general(wvt_ref[...], h, (((1,), (1,)), ((), ())),
                                          preferred_element_type=F32).astype(BF16)
        pu_ref[rows, :] = proj(V_COL + ATTN_W, POOL_CH).astype(BF16)
        off = V_COL + ATTN_W + POOL_CH
        for c in range(0, N_BRANCH * D_MODEL, 512):
            gate_ref[rows, c:c + 512] = _sigmoid(proj(off + c, 512)).astype(BF16)


def _proj_in(x2, g, w_bf, wvt_bf, layer):
    n = x2.shape[0]
    d_in = w_bf.shape[-1]
    row = lambda width: pl.BlockSpec((TM_PROJ, width), lambda i: (i, 0))
    widths = (CONV_CH, ATTN_W, ATTN_W, None, POOL_CH, N_BRANCH * D_MODEL)
    out_specs = [row(w) if w else pl.BlockSpec((ATTN_W, TM_PROJ), lambda i: (0, i)) for w in widths]
    out_shape = [jax.ShapeDtypeStruct((n, w) if w else (ATTN_W, n), BF16) for w in widths]
    return pl.pallas_call(
        _proj_in_kernel,
        grid=(n // TM_PROJ,),
        in_specs=[row(D_MODEL), _resident((1, D_MODEL), layer), _resident((D_MODEL, d_in), layer),
                  _resident((ATTN_W, D_MODEL), layer)],
        out_specs=out_specs,
        out_shape=out_shape,
        compiler_params=pltpu.CompilerParams(
            dimension_semantics=("arbitrary",), vmem_limit_bytes=VMEM_LIMIT),
        name="proj_in",
    )(x2, g, w_bf, wvt_bf)


def _rel_bucket_table(n_max):
    n = np.arange(n_max, dtype=np.int32)
    max_exact = REL_BUCKETS // 2
    nf = np.maximum(n, 1).astype(np.float32)
    large = max_exact + (np.log(nf / max_exact) / np.float32(math.log(REL_MAX_DIST / max_exact))
                         * (REL_BUCKETS - max_exact)).astype(np.int32)
    large = np.minimum(large, REL_BUCKETS - 1)
    return np.where(n < max_exact, n, large)


def _bias_block_nonzero(variant, delta):
    min_dist = variant * TK + delta * SUB - (SUB - 1)
    far_from = int(np.argmax(_rel_bucket_table(4 * TK) == REL_BUCKETS - 1))
    return min_dist < far_from


def _bias_block_index():
    nblk = TK // SUB
    needed = [(0, d) for d in range(nblk) if _bias_block_nonzero(0, d)]
    needed += [(1, d) for d in range(-(nblk - 1), nblk) if _bias_block_nonzero(1, d)]
    return {key: idx for idx, key in enumerate(needed)}


def _near_bias_blocks(rel_bias):
    assert (_rel_bucket_table(8 * TK)[TK + 1:] == REL_BUCKETS - 1).all()
    heads = rel_bias.shape[1]
    dist = np.arange(-(TK - 1), TK + TQ)
    bucket = _rel_bucket_table(TK + TQ)[np.maximum(dist, 0)]
    shifted = (rel_bias.astype(F32) - rel_bias[REL_BUCKETS - 1].astype(F32)) * LOG2E
    line = jnp.where(dist[None, :] >= 0, shifted[bucket].T, MASK_VALUE)
    win = 2 * SUB - 1
    starts = [variant * TK + (delta + TK // SUB - 1) * SUB for variant, delta in _bias_block_index()]
    w = jnp.stack([lax.slice_in_dim(line, s0, s0 + win, axis=1) for s0 in starts], axis=1)
    w = jnp.pad(w, ((0, 0), (0, 0), (0, 1)))
    m = jnp.tile(w, (1, 1, SUB))[:, :, :SUB * win].reshape(heads, len(starts), SUB, win)
    return m[:, :, :, SUB - 1:2 * SUB - 1]


def _attn_kernel(q_ref, k_ref, vt_ref, bias_ref, lamp_ref, g_ref, o_ref,
                 qb_ref, m_ref, acc_ref, s_ref, p_ref, *, lambda_init):
    nsub = TQ // SUB
    lp = lamp_ref[...]
    lam = (jnp.exp(jnp.sum(lp[0:1] * lp[1:2], axis=-1, keepdims=True))
           - jnp.exp(jnp.sum(lp[2:3] * lp[3:4], axis=-1, keepdims=True)) + lambda_init)

    def q_tile(i, carry):
        _attn_q_tile(i, lam, q_ref, k_ref, vt_ref, bias_ref, g_ref, o_ref,
                     qb_ref, m_ref, acc_ref, s_ref, p_ref, lambda_init)
        return carry

    lax.fori_loop(0, q_ref.shape[0] // TQ, q_tile, 0)


def _attn_q_tile(i, lam, q_ref, k_ref, vt_ref, bias_ref, g_ref, o_ref,
                 qb_ref, m_ref, acc_ref, s_ref, p_ref, lambda_init):
    nsub = TQ // SUB
    chains = [(ch, ch // nsub, ch % nsub) for ch in range(HEADS_PER_STEP * nsub)]
    hcols = lambda hh: slice(hh * ATTN_VD, (hh + 1) * ATTN_VD)
    bias_index = _bias_block_index()
    q0 = pl.multiple_of(i * TQ, TQ)
    lane = lax.broadcasted_iota(jnp.int32, (SUB, ATTN_VD), 1)
    for ch, hh, c in chains:
        qc = q_ref[pl.ds(q0 + c * SUB, SUB), hcols(hh)]
        zero = jnp.zeros_like(qc)
        qb_ref[ch, 0:SUB, :] = jnp.where(lane < ATTN_HD, qc, zero)
        qb_ref[ch, SUB:2 * SUB, :] = jnp.where(lane >= ATTN_HD, qc, zero)
        m_ref[ch] = jnp.full((1, 2 * SUB), MASK_VALUE, F32)
        acc_ref[ch] = jnp.zeros((ACC_ROWS, 2 * SUB), F32)

    def step(ks, n_keys, variant):
        tile_max = []
        for ch, hh, c in chains:
            nk = n_keys[c]
            s = lax.dot_general(k_ref[pl.ds(ks, nk), hcols(hh)], qb_ref[ch], (((1,), (1,)), ((), ())),
                                preferred_element_type=F32)
            if variant is not None:
                parts = []
                for bk in range(nk // SUB):
                    blk = s[bk * SUB:(bk + 1) * SUB, :]
                    if (variant, c - bk) in bias_index:
                        b = bias_ref[hh, bias_index[(variant, c - bk)]]
                        blk = blk + jnp.concatenate([b, b], axis=1)
                    parts.append(blk)
                s = jnp.concatenate(parts, axis=0)
            s_ref[ch, 0:nk, :] = s
            tile_max.append(jnp.max(s, axis=0, keepdims=True))
        alphas = []
        for ch, hh, c in chains:
            nk = n_keys[c]
            m_old = m_ref[ch]
            m_new = jnp.maximum(m_old, tile_max[ch])
            alpha = jnp.exp2(m_old - m_new)
            p_ref[ch, 0:nk, :] = jnp.exp2(s_ref[ch, 0:nk, :] - m_new).astype(BF16)
            m_ref[ch] = m_new
            alphas.append(alpha)
        for ch, hh, c in chains:
            nk = n_keys[c]
            vte = jnp.concatenate([vt_ref[hcols(hh), pl.ds(ks, nk)], jnp.ones((ONES_ROWS, nk), BF16)],
                                  axis=0)
            pv = jnp.dot(vte, p_ref[ch, 0:nk, :], preferred_element_type=F32)
            acc_ref[ch] = alphas[ch] * acc_ref[ch] + pv

    full = (TK,) * nsub

    def far(j, carry):
        step(pl.multiple_of(j * TK, TK), full, None)
        return carry

    lax.fori_loop(0, jnp.maximum(i - 1, 0), far, 0)

    @pl.when(i >= 1)
    def _():
        step(pl.multiple_of((i - 1) * TK, TK), full, 1)

    step(pl.multiple_of(i * TK, TK), tuple((c + 1) * SUB for c in range(nsub)), 0)

    for ch, hh, c in chains:
        acc = acc_ref[ch, 0:ATTN_VD, :]
        l = acc_ref[ch, ATTN_VD:ATTN_VD + 1, :]
        ot = acc[:, :SUB] / l[:, :SUB] - lam * (acc[:, SUB:] / l[:, SUB:])
        o = _rmsnorm(ot.T, g_ref[...]) * (1.0 - lambda_init)
        o_ref[pl.ds(q0 + c * SUB, SUB), hcols(hh)] = o.astype(BF16)


def _attention(q, k, vt, bias_blocks, lam_p, subln_g, batch, seq, layer):
    n = batch * seq
    nchain = HEADS_PER_STEP * (TQ // SUB)
    hw = HEADS_PER_STEP * ATTN_VD
    lambda_init = 0.8 - 0.6 * math.exp(-0.3 * layer)
    return pl.pallas_call(
        functools.partial(_attn_kernel, lambda_init=lambda_init),
        grid=(batch, ATTN_HEADS // HEADS_PER_STEP),
        in_specs=[
            pl.BlockSpec((seq, hw), lambda b, h: (b, h)),
            pl.BlockSpec((seq, hw), lambda b, h: (b, h)),
            pl.BlockSpec((hw, seq), lambda b, h: (h, b)),
            pl.BlockSpec((HEADS_PER_STEP, len(_bias_block_index()), SUB, SUB),
                         lambda b, h: (h, 0, 0, 0)),
            pl.BlockSpec((None, 4, ATTN_HD), lambda b, h: (layer, 0, 0)),
            pl.BlockSpec((None, 1, ATTN_VD), lambda b, h: (layer, 0, 0)),
        ],
        out_specs=pl.BlockSpec((seq, hw), lambda b, h: (b, h)),
        out_shape=jax.ShapeDtypeStruct((n, ATTN_W), BF16),
        scratch_shapes=[
            pltpu.VMEM((nchain, 2 * SUB, ATTN_VD), BF16),
            pltpu.VMEM((nchain, 1, 2 * SUB), F32),
            pltpu.VMEM((nchain, ACC_ROWS, 2 * SUB), F32),
            pltpu.VMEM((nchain, TK, 2 * SUB), F32),
            pltpu.VMEM((nchain, TK, 2 * SUB), BF16),
        ],
        compiler_params=pltpu.CompilerParams(
            dimension_semantics=("arbitrary", "arbitrary"),
            vmem_limit_bytes=VMEM_LIMIT),
        name="diff_attn",
    )(q, k, vt, bias_blocks, lam_p, subln_g)


def _mixer_kernel(x_ref, u_ref, uh_ref, pu_ref, ph_ref, o_ref, gate_ref,
                  dww_ref, dwb_ref, lng_ref, lnb_ref, wco_ref, wao_ref,
                  pw_ref, ps_ref, wpo_ref, wout_ref, gpost_ref,
                  out_ref,
                  ext_ref, pext_ref, cs_ref, pool_ref, merged_ref, *, tiles_per_seq):
    tm = TM_MIX
    t_in_seq = lax.rem(pl.program_id(0), tiles_per_seq)
    keep = t_in_seq > 0
    n_slab = CONV_CH // LANES

    def fill(dst_ref, tile_ref, halo_ref):
        for s in range(n_slab):
            cols = slice(s * LANES, (s + 1) * LANES)
            dst_ref[s, 0:HALO, :] = jnp.where(keep, halo_ref[:, cols].astype(F32), 0.0)
            dst_ref[s, HALO:HALO + tm, :] = tile_ref[:, cols].astype(F32)

    fill(ext_ref, u_ref, uh_ref)
    first_off = HALO - (CONV_WIDTH - 1)

    def conv_rows(base):
        conv = []
        for s in range(n_slab):
            cols = slice(s * LANES, (s + 1) * LANES)
            acc = jnp.zeros((CONV_RB, LANES), F32)
            for j in range(CONV_WIDTH):
                acc = acc + ext_ref[s, pl.ds(base + first_off + j, CONV_RB), :] * dww_ref[j:j + 1, cols]
            conv.append(acc + dwb_ref[:, cols])
        mu = sum(jnp.sum(a, axis=-1, keepdims=True) for a in conv) * (1.0 / CONV_CH)
        cen = [a - mu for a in conv]
        var = sum(jnp.sum(a * a, axis=-1, keepdims=True) for a in cen) * (1.0 / CONV_CH)
        inv = lax.rsqrt(var + EPS)
        for s in range(n_slab):
            cols = slice(s * LANES, (s + 1) * LANES)
            y = cen[s] * inv * lng_ref[:, cols] + lnb_ref[:, cols]
            cs_ref[pl.ds(base, CONV_RB), cols] = (y * _sigmoid(y)).astype(BF16)

    fill(pext_ref, pu_ref, ph_ref)
    pos = t_in_seq * tm + lax.broadcasted_iota(jnp.int32, (tm, POOL_GC), 0)
    for g, w in enumerate(POOL_WINDOWS):
        cols = slice(g * POOL_GC, (g + 1) * POOL_GC)
        e = pext_ref[g, HALO:HALO + tm, :]
        tot = e
        for j in range(1, w):
            tot = tot + pext_ref[g, HALO - j:HALO - j + tm, :]
        cnt = jnp.minimum(pos + 1, w).astype(F32)
        pooled = tot / cnt - e
        yg = jnp.dot(pooled.astype(BF16), pw_ref[g], preferred_element_type=F32)
        pool_ref[:, cols] = (yg * ps_ref[:, cols]).astype(BF16)

    nc = 256
    for r0 in range(0, tm, MIX_ROWS):
        rows = slice(r0, r0 + MIX_ROWS)
        for base in range(r0, r0 + MIX_ROWS, CONV_RB):
            conv_rows(base)
        for c in range(0, D_MODEL, nc):
            cols = slice(c, c + nc)
            y_conv = jnp.dot(cs_ref[rows, :], wco_ref[:, cols], preferred_element_type=F32)
            y_attn = jnp.dot(o_ref[rows, :], wao_ref[:, cols], preferred_element_type=F32)
            y_pool = jnp.dot(pool_ref[rows, :], wpo_ref[:, cols], preferred_element_type=F32)
            g0 = gate_ref[rows, c:c + nc].astype(F32)
            g1 = gate_ref[rows, D_MODEL + c:D_MODEL + c + nc].astype(F32)
            g2 = gate_ref[rows, 2 * D_MODEL + c:2 * D_MODEL + c + nc].astype(F32)
            merged_ref[rows, cols] = (g0 * y_conv + g1 * y_attn + g2 * y_pool).astype(BF16)
        mix = jnp.dot(merged_ref[rows, :], wout_ref[...], preferred_element_type=F32)
        out_ref[rows, :] = x_ref[rows, :] + _rmsnorm(mix, gpost_ref[...])


def _mixer(x2, u, pu, o, gates, dww, dwb, lng, lnb, wco, wao, pw, ps, wpo, wout, gpost, seq, layer):
    n = x2.shape[0]
    tm = TM_MIX
    tiles_per_seq = seq // tm
    row = lambda width: pl.BlockSpec((tm, width), lambda i: (i, 0))
    halo = lambda width: pl.BlockSpec(
        (HALO, width), lambda i: (jnp.maximum(i * (tm // HALO) - 1, 0), 0))
    return pl.pallas_call(
        functools.partial(_mixer_kernel, tiles_per_seq=tiles_per_seq),
        grid=(n // tm,),
        in_specs=[
            row(D_MODEL), row(CONV_CH), halo(CONV_CH), row(POOL_CH), halo(POOL_CH),
            row(ATTN_W), row(N_BRANCH * D_MODEL),
            _resident((CONV_WIDTH, CONV_CH), layer), _resident((1, CONV_CH), layer),
            _resident((1, CONV_CH), layer), _resident((1, CONV_CH), layer),
            _resident((CONV_CH, D_MODEL), layer), _resident((ATTN_W, D_MODEL), layer),
            _resident((len(POOL_WINDOWS), POOL_GC, POOL_GC), layer), _resident((1, POOL_CH), layer),
            _resident((POOL_CH, D_MODEL), layer), _resident((D_MODEL, D_MODEL), layer),
            _resident((1, D_MODEL), layer),
        ],
        out_specs=row(D_MODEL),
        out_shape=jax.ShapeDtypeStruct((n, D_MODEL), F32),
        scratch_shapes=[
            pltpu.VMEM((CONV_CH // LANES, tm + HALO, LANES), F32),
            pltpu.VMEM((POOL_CH // LANES, tm + HALO, LANES), F32),
            pltpu.VMEM((tm, CONV_CH), BF16),
            pltpu.VMEM((tm, POOL_CH), BF16),
            pltpu.VMEM((tm, D_MODEL), BF16),
        ],
        compiler_params=pltpu.CompilerParams(
            dimension_semantics=("arbitrary",), vmem_limit_bytes=VMEM_LIMIT),
        name="mixer",
    )(x2, u, u, pu, pu, o, gates, dww, dwb, lng, lnb, wco, wao, pw, ps, wpo, wout, gpost)


def _mlp_kernel(x_ref, p_ref, gpre_ref, w1_ref, w2_ref, gpost_ref, wpp_ref, wpg_ref,
                out_ref, hid_ref):
    x = x_ref[...]
    h = _rmsnorm(x, gpre_ref[...]).astype(BF16)
    fc = 512
    for c in range(0, D_FF, fc):
        a = jnp.dot(h, w1_ref[:, c:c + fc], preferred_element_type=F32)
        a = jnp.maximum(a, 0.0)
        hid_ref[:, c:c + fc] = (a * a).astype(BF16)
    f = jnp.dot(hid_ref[...], w2_ref[...], preferred_element_type=F32)
    x = x + _rmsnorm(f, gpost_ref[...])
    gate = _sigmoid(jnp.dot(x.astype(BF16), wpg_ref[...], preferred_element_type=F32))
    pe = jnp.dot(p_ref[...].astype(BF16), wpp_ref[...], preferred_element_type=F32)
    out_ref[...] = x + gate * pe


def _mlp(x2, p3, gpre, w1, w2, gpost, wpp, wpg, layer):
    n = x2.shape[0]
    tm = TM_MLP
    row = lambda width: pl.BlockSpec((tm, width), lambda i: (i, 0))
    return pl.pallas_call(
        _mlp_kernel,
        grid=(n // tm,),
        in_specs=[
            row(D_MODEL), pl.BlockSpec((None, tm, PLE_DIM), lambda i: (layer, i, 0)),
            _resident((1, D_MODEL), layer),
            _resident((D_MODEL, D_FF), layer), _resident((D_FF, D_MODEL), layer),
            _resident((1, D_MODEL), layer),
            _resident((PLE_DIM, D_MODEL), layer), _resident((D_MODEL, D_MODEL), layer),
        ],
        out_specs=row(D_MODEL),
        out_shape=jax.ShapeDtypeStruct((n, D_MODEL), F32),
        scratch_shapes=[pltpu.VMEM((tm, D_FF), BF16)],
        compiler_params=pltpu.CompilerParams(
            dimension_semantics=("arbitrary",), vmem_limit_bytes=VMEM_LIMIT),
        name="mlp_ple",
    )(x2, p3, gpre, w1, w2, gpost, wpp, wpg)


def kernel(x, p, rel_bias, g_pre_mix, w_in, conv_dw_w, conv_dw_b, conv_ln_g, conv_ln_b, w_conv_out, lam_p, subln_g, w_attn_out, pool_w, pool_scale, w_pool_out, w_out, g_post_mix, g_pre_mlp, w_mlp_in, w_mlp_out, g_post_mlp, w_ple_proj, w_ple_gate):
    batch, seq, d = x.shape
    depth = w_in.shape[0]
    assert d == D_MODEL and seq % TM_MIX == 0 and seq % TQ == 0 and TQ == TK
    n = batch * seq
    x2 = x.reshape(n, d)
    bias_blocks = _near_bias_blocks(rel_bias)
    rows = lambda a: a.reshape(depth, 1, -1)
    bf = lambda a: a.astype(BF16)
    w_in_bf = bf(w_in)
    wvt_bf = bf(jnp.swapaxes(lax.optimization_barrier(w_in[:, :, V_COL:V_COL + ATTN_W]), 1, 2))
    p3 = p.reshape(depth, n, PLE_DIM)
    mixer_params = (conv_dw_w.reshape(depth, CONV_WIDTH, CONV_CH), rows(conv_dw_b),
                    rows(conv_ln_g), rows(conv_ln_b), bf(w_conv_out), bf(w_attn_out),
                    bf(pool_w), rows(pool_scale), bf(w_pool_out), bf(w_out), rows(g_post_mix))
    mlp_params = (rows(g_pre_mlp), bf(w_mlp_in), bf(w_mlp_out), rows(g_post_mlp),
                  bf(w_ple_proj), bf(w_ple_gate))
    g_pre, subln = rows(g_pre_mix), rows(subln_g)
    for layer in range(depth):
        u, q, k, vt, pu, gates = _proj_in(x2, g_pre, w_in_bf, wvt_bf, layer)
        o = _attention(q, k, vt, bias_blocks, lam_p, subln, batch, seq, layer)
        x2 = _mixer(x2, u, pu, o, gates, *mixer_params, seq, layer)
        x2 = _mlp(x2, p3, *mlp_params, layer)
    return x2.reshape(batch, seq, d)
```

```python
import functools
import math

import jax
import jax.numpy as jnp
import numpy as np
from jax import lax
from jax.experimental import pallas as pl
from jax.experimental.pallas import tpu as pltpu

F32 = jnp.float32
BF16 = jnp.bfloat16

D_MODEL = 1024
PLE_DIM = 256
CONV_CH = 512
CONV_WIDTH = 31
ATTN_HEADS = 4
ATTN_HD = 64
ATTN_VD = 2 * ATTN_HD
ATTN_W = ATTN_HEADS * ATTN_VD
POOL_CH = 512
POOL_WINDOWS = (2, 4, 8, 16)
POOL_GC = POOL_CH // len(POOL_WINDOWS)
N_BRANCH = 3
D_FF = 4 * D_MODEL
REL_BUCKETS = 32
REL_MAX_DIST = 128
EPS = 1e-6
MASK_VALUE = -1e30

TM_PROJ = 1024
PROJ_ROWS = 256
TM_MIX = 512
TM_MLP = 1024
TQ = 1024
TK = 1024
SUB = 128
HEADS_PER_STEP = 2
ONES_ROWS = 16
ACC_ROWS = ATTN_VD + ONES_ROWS
LOG2E = math.log2(math.e)
HALO = 32
CONV_RB = 64
MIX_ROWS = 256
LANES = 128
SUBLANES = 8
VMEM_LIMIT = 60000 * 1024


def _sigmoid(x):
    return 0.5 * jnp.tanh(0.5 * x) + 0.5


def _rmsnorm(x, g):
    return x * lax.rsqrt(jnp.mean(x * x, axis=-1, keepdims=True) + EPS) * g


def _resident(shape, layer):
    index = (layer,) + (0,) * len(shape)
    return pl.BlockSpec((None,) + tuple(shape), lambda *_: index, pipeline_mode=pl.Buffered(1))


V_COL = 2 * CONV_CH + 2 * ATTN_W


def _proj_in_kernel(x_ref, g_ref, w_ref, wvt_ref, u_ref, q_ref, k_ref, vt_ref, pu_ref, gate_ref):
    for r0 in range(0, TM_PROJ, PROJ_ROWS):
        rows = slice(r0, r0 + PROJ_ROWS)
        h = _rmsnorm(x_ref[rows, :], g_ref[...]).astype(BF16)

        def proj(c0, width):
            return jnp.dot(h, w_ref[:, c0:c0 + width], preferred_element_type=F32)

        ca = proj(0, CONV_CH)
        cb = proj(CONV_CH, CONV_CH)
        u_ref[rows, :] = (ca * _sigmoid(cb)).astype(BF16)
        off = 2 * CONV_CH
        q_ref[rows, :] = (proj(off, ATTN_W) * (ATTN_HD ** -0.5 * LOG2E)).astype(BF16)
        k_ref[rows, :] = proj(off + ATTN_W, ATTN_W).astype(BF16)
        vt_ref[:, rows] = lax.dot_general(wvt_ref[...], h, (((1,), (1,)), ((), ())),
                                          preferred_element_type=F32).astype(BF16)
        pu_ref[rows, :] = proj(V_COL + ATTN_W, POOL_CH).astype(BF16)
        off = V_COL + ATTN_W + POOL_CH
        for c in range(0, N_BRANCH * D_MODEL, 512):
            gate_ref[rows, c:c + 512] = _sigmoid(proj(off + c, 512)).astype(BF16)


def _proj_in(x2, g, w_bf, wvt_bf, layer):
    n = x2.shape[0]
    d_in = w_bf.shape[-1]
    row = lambda width: pl.BlockSpec((TM_PROJ, width), lambda i: (i, 0))
    widths = (CONV_CH, ATTN_W, ATTN_W, None, POOL_CH, N_BRANCH * D_MODEL)
    out_specs = [row(w) if w else pl.BlockSpec((ATTN_W, TM_PROJ), lambda i: (0, i)) for w in widths]
    out_shape = [jax.ShapeDtypeStruct((n, w) if w else (ATTN_W, n), BF16) for w in widths]
    return pl.pallas_call(
        _proj_in_kernel,
        grid=(n // TM_PROJ,),
        in_specs=[row(D_MODEL), _resident((1, D_MODEL), layer), _resident((D_MODEL, d_in), layer),
                  _resident((ATTN_W, D_MODEL), layer)],
        out_specs=out_specs,
        out_shape=out_shape,
        compiler_params=pltpu.CompilerParams(
            dimension_semantics=("arbitrary",), vmem_limit_bytes=VMEM_LIMIT),
        name="proj_in",
    )(x2, g, w_bf, wvt_bf)


def _rel_bucket_table(n_max):
    n = np.arange(n_max, dtype=np.int32)
    max_exact = REL_BUCKETS // 2
    nf = np.maximum(n, 1).astype(np.float32)
    large = max_exact + (np.log(nf / max_exact) / np.float32(math.log(REL_MAX_DIST / max_exact))
                         * (REL_BUCKETS - max_exact)).astype(np.int32)
    large = np.minimum(large, REL_BUCKETS - 1)
    return np.where(n < max_exact, n, large)


def _bias_block_nonzero(variant, delta):
    min_dist = variant * TK + delta * SUB - (SUB - 1)
    far_from = int(np.argmax(_rel_bucket_table(4 * TK) == REL_BUCKETS - 1))
    return min_dist < far_from


def _bias_block_index():
    nblk = TK // SUB
    needed = [(0, d) for d in range(nblk) if _bias_block_nonzero(0, d)]
    needed += [(1, d) for d in range(-(nblk - 1), nblk) if _bias_block_nonzero(1, d)]
    return {key: idx for idx, key in enumerate(needed)}


def _near_bias_blocks(rel_bias):
    assert (_rel_bucket_table(8 * TK)[TK + 1:] == REL_BUCKETS - 1).all()
    heads = rel_bias.shape[1]
    dist = np.arange(-(TK - 1), TK + TQ)
    bucket = _rel_bucket_table(TK + TQ)[np.maximum(dist, 0)]
    shifted = (rel_bias.astype(F32) - rel_bias[REL_BUCKETS - 1].astype(F32)) * LOG2E
    line = jnp.where(dist[None, :] >= 0, shifted[bucket].T, MASK_VALUE)
    win = 2 * SUB - 1
    starts = [variant * TK + (delta + TK // SUB - 1) * SUB for variant, delta in _bias_block_index()]
    w = jnp.stack([lax.slice_in_dim(line, s0, s0 + win, axis=1) for s0 in starts], axis=1)
    w = jnp.pad(w, ((0, 0), (0, 0), (0, 1)))
    m = jnp.tile(w, (1, 1, SUB))[:, :, :SUB * win].reshape(heads, len(starts), SUB, win)
    return m[:, :, :, SUB - 1:2 * SUB - 1]


def _attn_kernel(q_ref, k_ref, vt_ref, bias_ref, lamp_ref, g_ref, o_ref,
                 qb_ref, m_ref, acc_ref, s_ref, p_ref, *, lambda_init):
    nsub = TQ // SUB
    lp = lamp_ref[...]
    lam = (jnp.exp(jnp.sum(lp[0:1] * lp[1:2], axis=-1, keepdims=True))
           - jnp.exp(jnp.sum(lp[2:3] * lp[3:4], axis=-1, keepdims=True)) + lambda_init)

    def q_tile(i, carry):
        _attn_q_tile(i, lam, q_ref, k_ref, vt_ref, bias_ref, g_ref, o_ref,
                     qb_ref, m_ref, acc_ref, s_ref, p_ref, lambda_init)
        return carry

    lax.fori_loop(0, q_ref.shape[0] // TQ, q_tile, 0)


def _attn_q_tile(i, lam, q_ref, k_ref, vt_ref, bias_ref, g_ref, o_ref,
                 qb_ref, m_ref, acc_ref, s_ref, p_ref, lambda_init):
    nsub = TQ // SUB
    chains = [(ch, ch // nsub, ch % nsub) for ch in range(HEADS_PER_STEP * nsub)]
    hcols = lambda hh: slice(hh * ATTN_VD, (hh + 1) * ATTN_VD)
    bias_index = _bias_block_index()
    q0 = pl.multiple_of(i * TQ, TQ)
    lane = lax.broadcasted_iota(jnp.int32, (SUB, ATTN_VD), 1)
    for ch, hh, c in chains:
        qc = q_ref[pl.ds(q0 + c * SUB, SUB), hcols(hh)]
        zero = jnp.zeros_like(qc)
        qb_ref[ch, 0:SUB, :] = jnp.where(lane < ATTN_HD, qc, zero)
        qb_ref[ch, SUB:2 * SUB, :] = jnp.where(lane >= ATTN_HD, qc, zero)
        m_ref[ch] = jnp.full((1, 2 * SUB), MASK_VALUE, F32)
        acc_ref[ch] = jnp.zeros((ACC_ROWS, 2 * SUB), F32)

    def step(ks, n_keys, variant):
        tile_max = []
        for ch, hh, c in chains:
            nk = n_keys[c]
            s = lax.dot_general(k_ref[pl.ds(ks, nk), hcols(hh)], qb_ref[ch], (((1,), (1,)), ((), ())),
                                preferred_element_type=F32)
            if variant is not None:
                parts = []
                for bk in range(nk // SUB):
                    blk = s[bk * SUB:(bk + 1) * SUB, :]
                    if (variant, c - bk) in bias_index:
                        b = bias_ref[hh, bias_index[(variant, c - bk)]]
                        blk = blk + jnp.concatenate([b, b], axis=1)
                    parts.append(blk)
                s = jnp.concatenate(parts, axis=0)
            s_ref[ch, 0:nk, :] = s
            tile_max.append(jnp.max(s, axis=0, keepdims=True))
        alphas = []
        for ch, hh, c in chains:
            nk = n_keys[c]
            m_old = m_ref[ch]
            m_new = jnp.maximum(m_old, tile_max[ch])
            alpha = jnp.exp2(m_old - m_new)
            p_ref[ch, 0:nk, :] = jnp.exp2(s_ref[ch, 0:nk, :] - m_new).astype(BF16)
            m_ref[ch] = m_new
            alphas.append(alpha)
        for ch, hh, c in chains:
            nk = n_keys[c]
            vte = jnp.concatenate([vt_ref[hcols(hh), pl.ds(ks, nk)], jnp.ones((ONES_ROWS, nk), BF16)],
                                  axis=0)
            pv = jnp.dot(vte, p_ref[ch, 0:nk, :], preferred_element_type=F32)
            acc_ref[ch] = alphas[ch] * acc_ref[ch] + pv

    full = (TK,) * nsub

    def far(j, carry):
        step(pl.multiple_of(j * TK, TK), full, None)
        return carry

    lax.fori_loop(0, jnp.maximum(i - 1, 0), far, 0)

    @pl.when(i >= 1)
    def _():
        step(pl.multiple_of((i - 1) * TK, TK), full, 1)

    step(pl.multiple_of(i * TK, TK), tuple((c + 1) * SUB for c in range(nsub)), 0)

    for ch, hh, c in chains:
        acc = acc_ref[ch, 0:ATTN_VD, :]
        l = acc_ref[ch, ATTN_VD:ATTN_VD + 1, :]
        ot = acc[:, :SUB] / l[:, :SUB] - lam * (acc[:, SUB:] / l[:, SUB:])
        o = _rmsnorm(ot.T, g_ref[...]) * (1.0 - lambda_init)
        o_ref[pl.ds(q0 + c * SUB, SUB), hcols(hh)] = o.astype(BF16)


def _attention(q, k, vt, bias_blocks, lam_p, subln_g, batch, seq, layer):
    n = batch * seq
    nchain = HEADS_PER_STEP * (TQ // SUB)
    hw = HEADS_PER_STEP * ATTN_VD
    lambda_init = 0.8 - 0.6 * math.exp(-0.3 * layer)
    return pl.pallas_call(
        functools.partial(_attn_kernel, lambda_init=lambda_init),
        grid=(batch, ATTN_HEADS // HEADS_PER_STEP),
        in_specs=[
            pl.BlockSpec((seq, hw), lambda b, h: (b, h)),
            pl.BlockSpec((seq, hw), lambda b, h: (b, h)),
            pl.BlockSpec((hw, seq), lambda b, h: (h, b)),
            pl.BlockSpec((HEADS_PER_STEP, len(_bias_block_index()), SUB, SUB),
                         lambda b, h: (h, 0, 0, 0)),
            pl.BlockSpec((None, 4, ATTN_HD), lambda b, h: (layer, 0, 0)),
            pl.BlockSpec((None, 1, ATTN_VD), lambda b, h: (layer, 0, 0)),
        ],
        out_specs=pl.BlockSpec((seq, hw), lambda b, h: (b, h)),
        out_shape=jax.ShapeDtypeStruct((n, ATTN_W), BF16),
        scratch_shapes=[
            pltpu.VMEM((nchain, 2 * SUB, ATTN_VD), BF16),
            pltpu.VMEM((nchain, 1, 2 * SUB), F32),
            pltpu.VMEM((nchain, ACC_ROWS, 2 * SUB), F32),
            pltpu.VMEM((nchain, TK, 2 * SUB), F32),
            pltpu.VMEM((nchain, TK, 2 * SUB), BF16),
        ],
        compiler_params=pltpu.CompilerParams(
            dimension_semantics=("arbitrary", "arbitrary"),
            vmem_limit_bytes=VMEM_LIMIT),
        name="diff_attn",
    )(q, k, vt, bias_blocks, lam_p, subln_g)


def _mixer_kernel(x_ref, u_ref, uh_ref, pu_ref, ph_ref, o_ref, gate_ref,
                  dww_ref, dwb_ref, lng_ref, lnb_ref, wco_ref, wao_ref,
                  pw_ref, ps_ref, wpo_ref, wout_ref, gpost_ref,
                  out_ref,
                  ext_ref, pext_ref, cs_ref, pool_ref, merged_ref, *, tiles_per_seq):
    tm = TM_MIX
    t_in_seq = lax.rem(pl.program_id(0), tiles_per_seq)
    keep = t_in_seq > 0
    n_slab = CONV_CH // LANES

    def fill(dst_ref, tile_ref, halo_ref):
        for s in range(n_slab):
            cols = slice(s * LANES, (s + 1) * LANES)
            dst_ref[s, 0:HALO, :] = jnp.where(keep, halo_ref[:, cols].astype(F32), 0.0)
            dst_ref[s, HALO:HALO + tm, :] = tile_ref[:, cols].astype(F32)

    fill(ext_ref.at[0], u_ref, uh_ref)
    span = tm + HALO - SUBLANES
    for r in range(1, SUBLANES):
        for s in range(n_slab):
            ext_ref[r, s, 0:span, :] = ext_ref[0, s, r:r + span, :]
    first_off = HALO - (CONV_WIDTH - 1)

    def conv_rows(base):
        conv = []
        for s in range(n_slab):
            cols = slice(s * LANES, (s + 1) * LANES)
            acc = jnp.zeros((CONV_RB, LANES), F32)
            for j in range(CONV_WIDTH):
                shift, aligned = (first_off + j) % SUBLANES, (first_off + j) // SUBLANES * SUBLANES
                acc = acc + ext_ref[shift, s, pl.ds(base + aligned, CONV_RB), :] * dww_ref[j:j + 1, cols]
            conv.append(acc + dwb_ref[:, cols])
        mu = sum(jnp.sum(a, axis=-1, keepdims=True) for a in conv) * (1.0 / CONV_CH)
        cen = [a - mu for a in conv]
        var = sum(jnp.sum(a * a, axis=-1, keepdims=True) for a in cen) * (1.0 / CONV_CH)
        inv = lax.rsqrt(var + EPS)
        for s in range(n_slab):
            cols = slice(s * LANES, (s + 1) * LANES)
            y = cen[s] * inv * lng_ref[:, cols] + lnb_ref[:, cols]
            cs_ref[pl.ds(base, CONV_RB), cols] = (y * _sigmoid(y)).astype(BF16)

    fill(pext_ref, pu_ref, ph_ref)
    pos = t_in_seq * tm + lax.broadcasted_iota(jnp.int32, (tm, POOL_GC), 0)
    for g, w in enumerate(POOL_WINDOWS):
        cols = slice(g * POOL_GC, (g + 1) * POOL_GC)
        e = pext_ref[g, HALO:HALO + tm, :]
        tot = e
        for j in range(1, w):
            tot = tot + pext_ref[g, HALO - j:HALO - j + tm, :]
        cnt = jnp.minimum(pos + 1, w).astype(F32)
        pooled = tot / cnt - e
        yg = jnp.dot(pooled.astype(BF16), pw_ref[g], preferred_element_type=F32)
        pool_ref[:, cols] = (yg * ps_ref[:, cols]).astype(BF16)

    nc = 256
    for r0 in range(0, tm, MIX_ROWS):
        rows = slice(r0, r0 + MIX_ROWS)
        for base in range(r0, r0 + MIX_ROWS, CONV_RB):
            conv_rows(base)
        for c in range(0, D_MODEL, nc):
            cols = slice(c, c + nc)
            y_conv = jnp.dot(cs_ref[rows, :], wco_ref[:, cols], preferred_element_type=F32)
            y_attn = jnp.dot(o_ref[rows, :], wao_ref[:, cols], preferred_element_type=F32)
            y_pool = jnp.dot(pool_ref[rows, :], wpo_ref[:, cols], preferred_element_type=F32)
            g0 = gate_ref[rows, c:c + nc].astype(F32)
            g1 = gate_ref[rows, D_MODEL + c:D_MODEL + c + nc].astype(F32)
            g2 = gate_ref[rows, 2 * D_MODEL + c:2 * D_MODEL + c + nc].astype(F32)
            merged_ref[rows, cols] = (g0 * y_conv + g1 * y_attn + g2 * y_pool).astype(BF16)
        mix = jnp.dot(merged_ref[rows, :], wout_ref[...], preferred_element_type=F32)
        out_ref[rows, :] = x_ref[rows, :] + _rmsnorm(mix, gpost_ref[...])


def _mixer(x2, u, pu, o, gates, dww, dwb, lng, lnb, wco, wao, pw, ps, wpo, wout, gpost, seq, layer):
    n = x2.shape[0]
    tm = TM_MIX
    tiles_per_seq = seq // tm
    row = lambda width: pl.BlockSpec((tm, width), lambda i: (i, 0))
    halo = lambda width: pl.BlockSpec(
        (HALO, width), lambda i: (jnp.maximum(i * (tm // HALO) - 1, 0), 0))
    return pl.pallas_call(
        functools.partial(_mixer_kernel, tiles_per_seq=tiles_per_seq),
        grid=(n // tm,),
        in_specs=[
            row(D_MODEL), row(CONV_CH), halo(CONV_CH), row(POOL_CH), halo(POOL_CH),
            row(ATTN_W), row(N_BRANCH * D_MODEL),
            _resident((CONV_WIDTH, CONV_CH), layer), _resident((1, CONV_CH), layer),
            _resident((1, CONV_CH), layer), _resident((1, CONV_CH), layer),
            _resident((CONV_CH, D_MODEL), layer), _resident((ATTN_W, D_MODEL), layer),
            _resident((len(POOL_WINDOWS), POOL_GC, POOL_GC), layer), _resident((1, POOL_CH), layer),
            _resident((POOL_CH, D_MODEL), layer), _resident((D_MODEL, D_MODEL), layer),
            _resident((1, D_MODEL), layer),
        ],
        out_specs=row(D_MODEL),
        out_shape=jax.ShapeDtypeStruct((n, D_MODEL), F32),
        scratch_shapes=[
            pltpu.VMEM((SUBLANES, CONV_CH // LANES, tm + HALO, LANES), F32),
            pltpu.VMEM((POOL_CH // LANES, tm + HALO, LANES), F32),
            pltpu.VMEM((tm, CONV_CH), BF16),
            pltpu.VMEM((tm, POOL_CH), BF16),
            pltpu.VMEM((tm, D_MODEL), BF16),
        ],
        compiler_params=pltpu.CompilerParams(
            dimension_semantics=("arbitrary",), vmem_limit_bytes=VMEM_LIMIT),
        name="mixer",
    )(x2, u, u, pu, pu, o, gates, dww, dwb, lng, lnb, wco, wao, pw, ps, wpo, wout, gpost)


def _mlp_kernel(x_ref, p_ref, gpre_ref, w1_ref, w2_ref, gpost_ref, wpp_ref, wpg_ref,
                out_ref, hid_ref):
    x = x_ref[...]
    h = _rmsnorm(x, gpre_ref[...]).astype(BF16)
    fc = 512
    for c in range(0, D_FF, fc):
        a = jnp.dot(h, w1_ref[:, c:c + fc], preferred_element_type=F32)
        a = jnp.maximum(a, 0.0)
        hid_ref[:, c:c + fc] = (a * a).astype(BF16)
    f = jnp.dot(hid_ref[...], w2_ref[...], preferred_element_type=F32)
    x = x + _rmsnorm(f, gpost_ref[...])
    gate = _sigmoid(jnp.dot(x.astype(BF16), wpg_ref[...], preferred_element_type=F32))
    pe = jnp.dot(p_ref[...].astype(BF16), wpp_ref[...], preferred_element_type=F32)
    out_ref[...] = x + gate * pe


def _mlp(x2, p3, gpre, w1, w2, gpost, wpp, wpg, layer):
    n = x2.shape[0]
    tm = TM_MLP
    row = lambda width: pl.BlockSpec((tm, width), lambda i: (i, 0))
    return pl.pallas_call(
        _mlp_kernel,
        grid=(n // tm,),
        in_specs=[
            row(D_MODEL), pl.BlockSpec((None, tm, PLE_DIM), lambda i: (layer, i, 0)),
            _resident((1, D_MODEL), layer),
            _resident((D_MODEL, D_FF), layer), _resident((D_FF, D_MODEL), layer),
            _resident((1, D_MODEL), layer),
            _resident((PLE_DIM, D_MODEL), layer), _resident((D_MODEL, D_MODEL), layer),
        ],
        out_specs=row(D_MODEL),
        out_shape=jax.ShapeDtypeStruct((n, D_MODEL), F32),
        scratch_shapes=[pltpu.VMEM((tm, D_FF), BF16)],
        compiler_params=pltpu.CompilerParams(
            dimension_semantics=("arbitrary",), vmem_limit_bytes=VMEM_LIMIT),
        name="mlp_ple",
    )(x2, p3, gpre, w1, w2, gpost, wpp, wpg)


def kernel(x, p, rel_bias, g_pre_mix, w_in, conv_dw_w, conv_dw_b, conv_ln_g, conv_ln_b, w_conv_out, lam_p, subln_g, w_attn_out, pool_w, pool_scale, w_pool_out, w_out, g_post_mix, g_pre_mlp, w_mlp_in, w_mlp_out, g_post_mlp, w_ple_proj, w_ple_gate):
    batch, seq, d = x.shape
    depth = w_in.shape[0]
    assert d == D_MODEL and seq % TM_MIX == 0 and seq % TQ == 0 and TQ == TK
    n = batch * seq
    x2 = x.reshape(n, d)
    bias_blocks = _near_bias_blocks(rel_bias)
    rows = lambda a: a.reshape(depth, 1, -1)
    bf = lambda a: a.astype(BF16)
    w_in_bf = bf(w_in)
    wvt_bf = bf(jnp.swapaxes(lax.optimization_barrier(w_in[:, :, V_COL:V_COL + ATTN_W]), 1, 2))
    p3 = p.reshape(depth, n, PLE_DIM)
    mixer_params = (conv_dw_w.reshape(depth, CONV_WIDTH, CONV_CH), rows(conv_dw_b),
                    rows(conv_ln_g), rows(conv_ln_b), bf(w_conv_out), bf(w_attn_out),
                    bf(pool_w), rows(pool_scale), bf(w_pool_out), bf(w_out), rows(g_post_mix))
    mlp_params = (rows(g_pre_mlp), bf(w_mlp_in), bf(w_mlp_out), rows(g_post_mlp),
                  bf(w_ple_proj), bf(w_ple_gate))
    g_pre, subln = rows(g_pre_mix), rows(subln_g)
    for layer in range(depth):
        u, q, k, vt, pu, gates = _proj_in(x2, g_pre, w_in_bf, wvt_bf, layer)
        o = _attention(q, k, vt, bias_blocks, lam_p, subln, batch, seq, layer)
        x2 = _mixer(x2, u, pu, o, gates, *mixer_params, seq, layer)
        x2 = _mlp(x2, p3, *mlp_params, layer)
    return x2.reshape(batch, seq, d)
```

```python
import functools
import math

import jax
import jax.numpy as jnp
import numpy as np
from jax import lax
from jax.experimental import pallas as pl
from jax.experimental.pallas import tpu as pltpu

F32 = jnp.float32
BF16 = jnp.bfloat16

D_MODEL = 1024
PLE_DIM = 256
CONV_CH = 512
CONV_WIDTH = 31
ATTN_HEADS = 4
ATTN_HD = 64
ATTN_VD = 2 * ATTN_HD
ATTN_W = ATTN_HEADS * ATTN_VD
POOL_CH = 512
POOL_WINDOWS = (2, 4, 8, 16)
POOL_GC = POOL_CH // len(POOL_WINDOWS)
N_BRANCH = 3
D_FF = 4 * D_MODEL
REL_BUCKETS = 32
REL_MAX_DIST = 128
EPS = 1e-6
MASK_VALUE = -1e30

TM_PROJ = 1024
PROJ_ROWS = 256
TM_MIX = 1024
TM_MLP = 1024
TQ = 2048
TK = 2048
SUB = 128
HEADS_PER_STEP = 1
ONES_ROWS = 16
ACC_ROWS = ATTN_VD + ONES_ROWS
LOG2E = math.log2(math.e)
HALO = 32
CONV_RB = 64
MIX_ROWS = 256
LANES = 128
VMEM_LIMIT = 60000 * 1024


def _sigmoid(x):
    return 0.5 * jnp.tanh(0.5 * x) + 0.5


def _rmsnorm(x, g):
    return x * lax.rsqrt(jnp.mean(x * x, axis=-1, keepdims=True) + EPS) * g


def _resident(shape, layer):
    index = (layer,) + (0,) * len(shape)
    return pl.BlockSpec((None,) + tuple(shape), lambda *_: index, pipeline_mode=pl.Buffered(1))


V_COL = 2 * CONV_CH + 2 * ATTN_W


def _proj_in_kernel(x_ref, g_ref, w_ref, wvt_ref, u_ref, q_ref, k_ref, vt_ref, pu_ref, gate_ref):
    for r0 in range(0, TM_PROJ, PROJ_ROWS):
        rows = slice(r0, r0 + PROJ_ROWS)
        h = _rmsnorm(x_ref[rows, :], g_ref[...]).astype(BF16)

        def proj(c0, width):
            return jnp.dot(h, w_ref[:, c0:c0 + width], preferred_element_type=F32)

        ca = proj(0, CONV_CH)
        cb = proj(CONV_CH, CONV_CH)
        u_ref[rows, :] = (ca * _sigmoid(cb)).astype(BF16)
        off = 2 * CONV_CH
        q_ref[rows, :] = (proj(off, ATTN_W) * (ATTN_HD ** -0.5 * LOG2E)).astype(BF16)
        k_ref[rows, :] = proj(off + ATTN_W, ATTN_W).astype(BF16)
        vt_ref[:, rows] = lax.dot_general(wvt_ref[...], h, (((1,), (1,)), ((), ())),
                                          preferred_element_type=F32).astype(BF16)
        pu_ref[rows, :] = proj(V_COL + ATTN_W, POOL_CH).astype(BF16)
        off = V_COL + ATTN_W + POOL_CH
        for c in range(0, N_BRANCH * D_MODEL, 512):
            gate_ref[rows, c:c + 512] = _sigmoid(proj(off + c, 512)).astype(BF16)


def _proj_in(x2, g, w_bf, wvt_bf, layer):
    n = x2.shape[0]
    d_in = w_bf.shape[-1]
    row = lambda width: pl.BlockSpec((TM_PROJ, width), lambda i: (i, 0))
    widths = (CONV_CH, ATTN_W, ATTN_W, None, POOL_CH, N_BRANCH * D_MODEL)
    out_specs = [row(w) if w else pl.BlockSpec((ATTN_W, TM_PROJ), lambda i: (0, i)) for w in widths]
    out_shape = [jax.ShapeDtypeStruct((n, w) if w else (ATTN_W, n), BF16) for w in widths]
    return pl.pallas_call(
        _proj_in_kernel,
        grid=(n // TM_PROJ,),
        in_specs=[row(D_MODEL), _resident((1, D_MODEL), layer), _resident((D_MODEL, d_in), layer),
                  _resident((ATTN_W, D_MODEL), layer)],
        out_specs=out_specs,
        out_shape=out_shape,
        compiler_params=pltpu.CompilerParams(
            dimension_semantics=("arbitrary",), vmem_limit_bytes=VMEM_LIMIT),
        name="proj_in",
    )(x2, g, w_bf, wvt_bf)


def _rel_bucket_table(n_max):
    n = np.arange(n_max, dtype=np.int32)
    max_exact = REL_BUCKETS // 2
    nf = np.maximum(n, 1).astype(np.float32)
    large = max_exact + (np.log(nf / max_exact) / np.float32(math.log(REL_MAX_DIST / max_exact))
                         * (REL_BUCKETS - max_exact)).astype(np.int32)
    large = np.minimum(large, REL_BUCKETS - 1)
    return np.where(n < max_exact, n, large)


def _bias_block_nonzero(variant, delta):
    min_dist = variant * TK + delta * SUB - (SUB - 1)
    far_from = int(np.argmax(_rel_bucket_table(4 * TK) == REL_BUCKETS - 1))
    return min_dist < far_from


def _bias_block_index():
    nblk = TK // SUB
    needed = [(0, d) for d in range(nblk) if _bias_block_nonzero(0, d)]
    needed += [(1, d) for d in range(-(nblk - 1), nblk) if _bias_block_nonzero(1, d)]
    return {key: idx for idx, key in enumerate(needed)}


def _near_bias_blocks(rel_bias):
    assert (_rel_bucket_table(8 * TK)[TK + 1:] == REL_BUCKETS - 1).all()
    heads = rel_bias.shape[1]
    dist = np.arange(-(TK - 1), TK + TQ)
    bucket = _rel_bucket_table(TK + TQ)[np.maximum(dist, 0)]
    shifted = (rel_bias.astype(F32) - rel_bias[REL_BUCKETS - 1].astype(F32)) * LOG2E
    line = jnp.where(dist[None, :] >= 0, shifted[bucket].T, MASK_VALUE)
    win = 2 * SUB - 1
    starts = [variant * TK + (delta + TK // SUB - 1) * SUB for variant, delta in _bias_block_index()]
    w = jnp.stack([lax.slice_in_dim(line, s0, s0 + win, axis=1) for s0 in starts], axis=1)
    w = jnp.pad(w, ((0, 0), (0, 0), (0, 1)))
    m = jnp.tile(w, (1, 1, SUB))[:, :, :SUB * win].reshape(heads, len(starts), SUB, win)
    return m[:, :, :, SUB - 1:2 * SUB - 1]


def _chain_rows(seq):
    return tuple((c + 1) * SUB if seq == TQ else TK for c in range(TQ // SUB))


def _attn_kernel(q_ref, k_ref, vt_ref, bias_ref, lamp_ref, g_ref, o_ref,
                 qb_ref, m_ref, acc_ref, s_ref, p_ref, *, lambda_init):
    lp = lamp_ref[...]
    lam = (jnp.exp(jnp.sum(lp[0:1] * lp[1:2], axis=-1, keepdims=True))
           - jnp.exp(jnp.sum(lp[2:3] * lp[3:4], axis=-1, keepdims=True)) + lambda_init)
    refs = (q_ref, k_ref, vt_ref, bias_ref, g_ref, o_ref, qb_ref, m_ref, acc_ref, s_ref, p_ref)
    n_tiles = q_ref.shape[0] // TQ
    if n_tiles == 1:
        _attn_q_tile(0, lam, *refs, lambda_init)
    else:
        def q_tile(i, carry):
            _attn_q_tile(i, lam, *refs, lambda_init)
            return carry

        lax.fori_loop(0, n_tiles, q_tile, 0)


def _attn_q_tile(i, lam, q_ref, k_ref, vt_ref, bias_ref, g_ref, o_ref,
                 qb_ref, m_ref, acc_ref, s_ref, p_ref, lambda_init):
    nsub = TQ // SUB
    single = isinstance(i, int)
    row0 = np.concatenate([[0], np.cumsum(_chain_rows(TQ if single else None))]).tolist()
    crow = lambda c, nk: slice(row0[c], row0[c] + nk)
    chains = [(ch, ch // nsub, ch % nsub) for ch in range(HEADS_PER_STEP * nsub)]
    hcols = lambda hh: slice(hh * ATTN_VD, (hh + 1) * ATTN_VD)
    bias_index = _bias_block_index()
    q0 = 0 if single else pl.multiple_of(i * TQ, TQ)
    lane = lax.broadcasted_iota(jnp.int32, (SUB, ATTN_VD), 1)
    for ch, hh, c in chains:
        qc = q_ref[pl.ds(q0 + c * SUB, SUB), hcols(hh)]
        zero = jnp.zeros_like(qc)
        qb_ref[ch, 0:SUB, :] = jnp.where(lane < ATTN_HD, qc, zero)
        qb_ref[ch, SUB:2 * SUB, :] = jnp.where(lane >= ATTN_HD, qc, zero)
        if not single:
            m_ref[ch] = jnp.full((1, 2 * SUB), MASK_VALUE, F32)
            acc_ref[ch] = jnp.zeros((ACC_ROWS, 2 * SUB), F32)

    def step(ks, n_keys, variant):
        tile_max = []
        for ch, hh, c in chains:
            nk = n_keys[c]
            s = lax.dot_general(k_ref[pl.ds(ks, nk), hcols(hh)], qb_ref[ch], (((1,), (1,)), ((), ())),
                                preferred_element_type=F32)
            if variant is not None:
                parts = []
                for bk in range(nk // SUB):
                    blk = s[bk * SUB:(bk + 1) * SUB, :]
                    if (variant, c - bk) in bias_index:
                        b = bias_ref[hh, bias_index[(variant, c - bk)]]
                        blk = blk + jnp.concatenate([b, b], axis=1)
                    parts.append(blk)
                s = jnp.concatenate(parts, axis=0)
            s_ref[hh, crow(c, nk), :] = s
            tile_max.append(jnp.max(s, axis=0, keepdims=True))
        alphas = []
        for ch, hh, c in chains:
            nk = n_keys[c]
            if single:
                m_new = tile_max[ch]
            else:
                m_old = m_ref[ch]
                m_new = jnp.maximum(m_old, tile_max[ch])
                alphas.append(jnp.exp2(m_old - m_new))
                m_ref[ch] = m_new
            p_ref[hh, crow(c, nk), :] = jnp.exp2(s_ref[hh, crow(c, nk), :] - m_new).astype(BF16)
        for ch, hh, c in chains:
            nk = n_keys[c]
            vte = jnp.concatenate([vt_ref[hcols(hh), pl.ds(ks, nk)], jnp.ones((ONES_ROWS, nk), BF16)],
                                  axis=0)
            pv = jnp.dot(vte, p_ref[hh, crow(c, nk), :], preferred_element_type=F32)
            acc_ref[ch] = pv if single else alphas[ch] * acc_ref[ch] + pv

    full = (TK,) * nsub

    def far(j, carry):
        step(pl.multiple_of(j * TK, TK), full, None)
        return carry

    if not single:
        lax.fori_loop(0, jnp.maximum(i - 1, 0), far, 0)

        @pl.when(i >= 1)
        def _():
            step(pl.multiple_of((i - 1) * TK, TK), full, 1)

    step(0 if single else pl.multiple_of(i * TK, TK), tuple((c + 1) * SUB for c in range(nsub)), 0)

    for ch, hh, c in chains:
        acc = acc_ref[ch, 0:ATTN_VD, :]
        l = acc_ref[ch, ATTN_VD:ATTN_VD + 1, :]
        ot = acc[:, :SUB] / l[:, :SUB] - lam * (acc[:, SUB:] / l[:, SUB:])
        o = _rmsnorm(ot.T, g_ref[...]) * (1.0 - lambda_init)
        o_ref[pl.ds(q0 + c * SUB, SUB), hcols(hh)] = o.astype(BF16)


def _attention(q, k, vt, bias_blocks, lam_p, subln_g, batch, seq, layer):
    n = batch * seq
    nchain = HEADS_PER_STEP * (TQ // SUB)
    hw = HEADS_PER_STEP * ATTN_VD
    lambda_init = 0.8 - 0.6 * math.exp(-0.3 * layer)
    return pl.pallas_call(
        functools.partial(_attn_kernel, lambda_init=lambda_init),
        grid=(batch, ATTN_HEADS // HEADS_PER_STEP),
        in_specs=[
            pl.BlockSpec((seq, hw), lambda b, h: (b, h)),
            pl.BlockSpec((seq, hw), lambda b, h: (b, h)),
            pl.BlockSpec((hw, seq), lambda b, h: (h, b)),
            pl.BlockSpec((HEADS_PER_STEP, len(_bias_block_index()), SUB, SUB),
                         lambda b, h: (h, 0, 0, 0)),
            pl.BlockSpec((None, 4, ATTN_HD), lambda b, h: (layer, 0, 0)),
            pl.BlockSpec((None, 1, ATTN_VD), lambda b, h: (layer, 0, 0)),
        ],
        out_specs=pl.BlockSpec((seq, hw), lambda b, h: (b, h)),
        out_shape=jax.ShapeDtypeStruct((n, ATTN_W), BF16),
        scratch_shapes=[
            pltpu.VMEM((nchain, 2 * SUB, ATTN_VD), BF16),
            pltpu.VMEM((nchain, 1, 2 * SUB), F32),
            pltpu.VMEM((nchain, ACC_ROWS, 2 * SUB), F32),
            pltpu.VMEM((HEADS_PER_STEP, sum(_chain_rows(seq)), 2 * SUB), F32),
            pltpu.VMEM((HEADS_PER_STEP, sum(_chain_rows(seq)), 2 * SUB), BF16),
        ],
        compiler_params=pltpu.CompilerParams(
            dimension_semantics=("arbitrary", "arbitrary"),
            vmem_limit_bytes=VMEM_LIMIT),
        name="diff_attn",
    )(q, k, vt, bias_blocks, lam_p, subln_g)


def _mixer_kernel(x_ref, u_ref, uh_ref, pu_ref, ph_ref, o_ref, gate_ref,
                  dww_ref, dwb_ref, lng_ref, lnb_ref, wco_ref, wao_ref,
                  pw_ref, ps_ref, wpo_ref, wout_ref, gpost_ref,
                  out_ref,
                  ext_ref, pext_ref, cs_ref, pool_ref, merged_ref, *, tiles_per_seq):
    tm = TM_MIX
    t_in_seq = lax.rem(pl.program_id(0), tiles_per_seq)
    keep = t_in_seq > 0
    n_slab = CONV_CH // LANES

    def fill(dst_ref, tile_ref, halo_ref):
        for s in range(n_slab):
            cols = slice(s * LANES, (s + 1) * LANES)
            dst_ref[s, 0:HALO, :] = jnp.where(keep, halo_ref[:, cols].astype(F32), 0.0)
            dst_ref[s, HALO:HALO + tm, :] = tile_ref[:, cols].astype(F32)

    fill(ext_ref, u_ref, uh_ref)
    first_off = HALO - (CONV_WIDTH - 1)

    def conv_rows(base):
        conv = []
        for s in range(n_slab):
            cols = slice(s * LANES, (s + 1) * LANES)
            acc = jnp.zeros((CONV_RB, LANES), F32)
            for j in range(CONV_WIDTH):
                acc = acc + ext_ref[s, pl.ds(base + first_off + j, CONV_RB), :] * dww_ref[j:j + 1, cols]
            conv.append(acc + dwb_ref[:, cols])
        mu = sum(jnp.sum(a, axis=-1, keepdims=True) for a in conv) * (1.0 / CONV_CH)
        cen = [a - mu for a in conv]
        var = sum(jnp.sum(a * a, axis=-1, keepdims=True) for a in cen) * (1.0 / CONV_CH)
        inv = lax.rsqrt(var + EPS)
        for s in range(n_slab):
            cols = slice(s * LANES, (s + 1) * LANES)
            y = cen[s] * inv * lng_ref[:, cols] + lnb_ref[:, cols]
            cs_ref[pl.ds(base, CONV_RB), cols] = (y * _sigmoid(y)).astype(BF16)

    fill(pext_ref, pu_ref, ph_ref)
    pos = t_in_seq * tm + lax.broadcasted_iota(jnp.int32, (tm, POOL_GC), 0)
    for g, w in enumerate(POOL_WINDOWS):
        cols = slice(g * POOL_GC, (g + 1) * POOL_GC)
        e = pext_ref[g, HALO:HALO + tm, :]
        tot = e
        for j in range(1, w):
            tot = tot + pext_ref[g, HALO - j:HALO - j + tm, :]
        cnt = jnp.minimum(pos + 1, w).astype(F32)
        pooled = tot / cnt - e
        yg = jnp.dot(pooled.astype(BF16), pw_ref[g], preferred_element_type=F32)
        pool_ref[:, cols] = (yg * ps_ref[:, cols]).astype(BF16)

    nc = 256
    for r0 in range(0, tm, MIX_ROWS):
        rows = slice(r0, r0 + MIX_ROWS)
        for base in range(r0, r0 + MIX_ROWS, CONV_RB):
            conv_rows(base)
        for c in range(0, D_MODEL, nc):
            cols = slice(c, c + nc)
            y_conv = jnp.dot(cs_ref[rows, :], wco_ref[:, cols], preferred_element_type=F32)
            y_attn = jnp.dot(o_ref[rows, :], wao_ref[:, cols], preferred_element_type=F32)
            y_pool = jnp.dot(pool_ref[rows, :], wpo_ref[:, cols], preferred_element_type=F32)
            g0 = gate_ref[rows, c:c + nc].astype(F32)
            g1 = gate_ref[rows, D_MODEL + c:D_MODEL + c + nc].astype(F32)
            g2 = gate_ref[rows, 2 * D_MODEL + c:2 * D_MODEL + c + nc].astype(F32)
            merged_ref[rows, cols] = (g0 * y_conv + g1 * y_attn + g2 * y_pool).astype(BF16)
        mix = jnp.dot(merged_ref[rows, :], wout_ref[...], preferred_element_type=F32)
        out_ref[rows, :] = x_ref[rows, :] + _rmsnorm(mix, gpost_ref[...])


def _mixer(x2, u, pu, o, gates, dww, dwb, lng, lnb, wco, wao, pw, ps, wpo, wout, gpost, seq, layer):
    n = x2.shape[0]
    tm = TM_MIX
    tiles_per_seq = seq // tm
    row = lambda width: pl.BlockSpec((tm, width), lambda i: (i, 0))
    halo = lambda width: pl.BlockSpec(
        (HALO, width), lambda i: (jnp.maximum(i * (tm // HALO) - 1, 0), 0))
    return pl.pallas_call(
        functools.partial(_mixer_kernel, tiles_per_seq=tiles_per_seq),
        grid=(n // tm,),
        in_specs=[
            row(D_MODEL), row(CONV_CH), halo(CONV_CH), row(POOL_CH), halo(POOL_CH),
            row(ATTN_W), row(N_BRANCH * D_MODEL),
            _resident((CONV_WIDTH, CONV_CH), layer), _resident((1, CONV_CH), layer),
            _resident((1, CONV_CH), layer), _resident((1, CONV_CH), layer),
            _resident((CONV_CH, D_MODEL), layer), _resident((ATTN_W, D_MODEL), layer),
            _resident((len(POOL_WINDOWS), POOL_GC, POOL_GC), layer), _resident((1, POOL_CH), layer),
            _resident((POOL_CH, D_MODEL), layer), _resident((D_MODEL, D_MODEL), layer),
            _resident((1, D_MODEL), layer),
        ],
        out_specs=row(D_MODEL),
        out_shape=jax.ShapeDtypeStruct((n, D_MODEL), F32),
        scratch_shapes=[
            pltpu.VMEM((CONV_CH // LANES, tm + HALO, LANES), F32),
            pltpu.VMEM((POOL_CH // LANES, tm + HALO, LANES), F32),
            pltpu.VMEM((tm, CONV_CH), BF16),
            pltpu.VMEM((tm, POOL_CH), BF16),
            pltpu.VMEM((tm, D_MODEL), BF16),
        ],
        compiler_params=pltpu.CompilerParams(
            dimension_semantics=("arbitrary",), vmem_limit_bytes=VMEM_LIMIT),
        name="mixer",
    )(x2, u, u, pu, pu, o, gates, dww, dwb, lng, lnb, wco, wao, pw, ps, wpo, wout, gpost)


def _mlp_kernel(x_ref, p_ref, gpre_ref, w1_ref, w2_ref, gpost_ref, wpp_ref, wpg_ref,
                out_ref, hid_ref):
    x = x_ref[...]
    h = _rmsnorm(x, gpre_ref[...]).astype(BF16)
    fc = 512
    for c in range(0, D_FF, fc):
        a = jnp.dot(h, w1_ref[:, c:c + fc], preferred_element_type=F32)
        a = jnp.maximum(a, 0.0)
        hid_ref[:, c:c + fc] = (a * a).astype(BF16)
    f = jnp.dot(hid_ref[...], w2_ref[...], preferred_element_type=F32)
    x = x + _rmsnorm(f, gpost_ref[...])
    gate = _sigmoid(jnp.dot(x.astype(BF16), wpg_ref[...], preferred_element_type=F32))
    pe = jnp.dot(p_ref[...].astype(BF16), wpp_ref[...], preferred_element_type=F32)
    out_ref[...] = x + gate * pe


def _mlp(x2, p3, gpre, w1, w2, gpost, wpp, wpg, layer):
    n = x2.shape[0]
    tm = TM_MLP
    row = lambda width: pl.BlockSpec((tm, width), lambda i: (i, 0))
    return pl.pallas_call(
        _mlp_kernel,
        grid=(n // tm,),
        in_specs=[
            row(D_MODEL), pl.BlockSpec((None, tm, PLE_DIM), lambda i: (layer, i, 0)),
            _resident((1, D_MODEL), layer),
            _resident((D_MODEL, D_FF), layer), _resident((D_FF, D_MODEL), layer),
            _resident((1, D_MODEL), layer),
            _resident((PLE_DIM, D_MODEL), layer), _resident((D_MODEL, D_MODEL), layer),
        ],
        out_specs=row(D_MODEL),
        out_shape=jax.ShapeDtypeStruct((n, D_MODEL), F32),
        scratch_shapes=[pltpu.VMEM((tm, D_FF), BF16)],
        compiler_params=pltpu.CompilerParams(
            dimension_semantics=("arbitrary",), vmem_limit_bytes=VMEM_LIMIT),
        name="mlp_ple",
    )(x2, p3, gpre, w1, w2, gpost, wpp, wpg)


def kernel(x, p, rel_bias, g_pre_mix, w_in, conv_dw_w, conv_dw_b, conv_ln_g, conv_ln_b, w_conv_out, lam_p, subln_g, w_attn_out, pool_w, pool_scale, w_pool_out, w_out, g_post_mix, g_pre_mlp, w_mlp_in, w_mlp_out, g_post_mlp, w_ple_proj, w_ple_gate):
    batch, seq, d = x.shape
    depth = w_in.shape[0]
    assert d == D_MODEL and seq % TM_MIX == 0 and seq % TQ == 0 and TQ == TK
    n = batch * seq
    x2 = x.reshape(n, d)
    bias_blocks = _near_bias_blocks(rel_bias)
    rows = lambda a: a.reshape(depth, 1, -1)
    bf = lambda a: a.astype(BF16)
    w_in_bf = bf(w_in)
    wvt_bf = bf(jnp.swapaxes(lax.optimization_barrier(w_in[:, :, V_COL:V_COL + ATTN_W]), 1, 2))
    p3 = p.reshape(depth, n, PLE_DIM)
    mixer_params = (conv_dw_w.reshape(depth, CONV_WIDTH, CONV_CH), rows(conv_dw_b),
                    rows(conv_ln_g), rows(conv_ln_b), bf(w_conv_out), bf(w_attn_out),
                    bf(pool_w), rows(pool_scale), bf(w_pool_out), bf(w_out), rows(g_post_mix))
    mlp_params = (rows(g_pre_mlp), bf(w_mlp_in), bf(w_mlp_out), rows(g_post_mlp),
                  bf(w_ple_proj), bf(w_ple_gate))
    g_pre, subln = rows(g_pre_mix), rows(subln_g)
    for layer in range(depth):
        u, q, k, vt, pu, gates = _proj_in(x2, g_pre, w_in_bf, wvt_bf, layer)
        o = _attention(q, k, vt, bias_blocks, lam_p, subln, batch, seq, layer)
        x2 = _mixer(x2, u, pu, o, gates, *mixer_params, seq, layer)
        x2 = _mlp(x2, p3, *mlp_params, layer)
    return x2.reshape(batch, seq, d)
```

```python
import functools
import math

import jax
import jax.numpy as jnp
import numpy as np
from jax import lax
from jax.experimental import pallas as pl
from jax.experimental.pallas import tpu as pltpu

F32 = jnp.float32
BF16 = jnp.bfloat16

D_MODEL = 1024
PLE_DIM = 256
CONV_CH = 512
CONV_WIDTH = 31
ATTN_HEADS = 4
ATTN_HD = 64
ATTN_VD = 2 * ATTN_HD
ATTN_W = ATTN_HEADS * ATTN_VD
POOL_CH = 512
POOL_WINDOWS = (2, 4, 8, 16)
POOL_GC = POOL_CH // len(POOL_WINDOWS)
N_BRANCH = 3
D_FF = 4 * D_MODEL
REL_BUCKETS = 32
REL_MAX_DIST = 128
EPS = 1e-6
MASK_VALUE = -1e30

TM_PROJ = 1024
PROJ_ROWS = 256
TM_MIX = 1024
TM_MLP = 1024
TQ = 2048
TK = 2048
SUB = 128
HEADS_PER_STEP = 1
ONES_ROWS = 16
ACC_ROWS = ATTN_VD + ONES_ROWS
LOG2E = math.log2(math.e)
HALO = 32
CONV_RB = 64
MIX_ROWS = 256
LANES = 128
VMEM_LIMIT = 60000 * 1024


def _sigmoid(x):
    return 0.5 * jnp.tanh(0.5 * x) + 0.5


def _rmsnorm(x, g):
    return x * lax.rsqrt(jnp.mean(x * x, axis=-1, keepdims=True) + EPS) * g


def _resident(shape, layer):
    index = (layer,) + (0,) * len(shape)
    return pl.BlockSpec((None,) + tuple(shape), lambda *_: index, pipeline_mode=pl.Buffered(1))


V_COL = 2 * CONV_CH + 2 * ATTN_W


def _proj_in_kernel(x_ref, g_ref, w_ref, wvt_ref, u_ref, q_ref, k_ref, vt_ref, pu_ref, gate_ref):
    for r0 in range(0, TM_PROJ, PROJ_ROWS):
        rows = slice(r0, r0 + PROJ_ROWS)
        h = _rmsnorm(x_ref[rows, :], g_ref[...]).astype(BF16)

        def proj(c0, width):
            return jnp.dot(h, w_ref[:, c0:c0 + width], preferred_element_type=F32)

        ca = proj(0, CONV_CH)
        cb = proj(CONV_CH, CONV_CH)
        u_ref[rows, :] = (ca * _sigmoid(cb)).astype(BF16)
        off = 2 * CONV_CH
        q_ref[rows, :] = (proj(off, ATTN_W) * (ATTN_HD ** -0.5 * LOG2E)).astype(BF16)
        k_ref[rows, :] = proj(off + ATTN_W, ATTN_W).astype(BF16)
        vt_ref[:, rows] = lax.dot_general(wvt_ref[...], h, (((1,), (1,)), ((), ())),
                                          preferred_element_type=F32).astype(BF16)
        pu_ref[rows, :] = proj(V_COL + ATTN_W, POOL_CH).astype(BF16)
        off = V_COL + ATTN_W + POOL_CH
        for c in range(0, N_BRANCH * D_MODEL, 512):
            gate_ref[rows, c:c + 512] = _sigmoid(proj(off + c, 512)).astype(BF16)


def _proj_in(x2, g, w_bf, wvt_bf, layer):
    n = x2.shape[0]
    d_in = w_bf.shape[-1]
    row = lambda width: pl.BlockSpec((TM_PROJ, width), lambda i: (i, 0))
    widths = (CONV_CH, ATTN_W, ATTN_W, None, POOL_CH, N_BRANCH * D_MODEL)
    out_specs = [row(w) if w else pl.BlockSpec((ATTN_W, TM_PROJ), lambda i: (0, i)) for w in widths]
    out_shape = [jax.ShapeDtypeStruct((n, w) if w else (ATTN_W, n), BF16) for w in widths]
    return pl.pallas_call(
        _proj_in_kernel,
        grid=(n // TM_PROJ,),
        in_specs=[row(D_MODEL), _resident((1, D_MODEL), layer), _resident((D_MODEL, d_in), layer),
                  _resident((ATTN_W, D_MODEL), layer)],
        out_specs=out_specs,
        out_shape=out_shape,
        compiler_params=pltpu.CompilerParams(
            dimension_semantics=("arbitrary",), vmem_limit_bytes=VMEM_LIMIT),
        name="proj_in",
    )(x2, g, w_bf, wvt_bf)


def _rel_bucket_table(n_max):
    n = np.arange(n_max, dtype=np.int32)
    max_exact = REL_BUCKETS // 2
    nf = np.maximum(n, 1).astype(np.float32)
    large = max_exact + (np.log(nf / max_exact) / np.float32(math.log(REL_MAX_DIST / max_exact))
                         * (REL_BUCKETS - max_exact)).astype(np.int32)
    large = np.minimum(large, REL_BUCKETS - 1)
    return np.where(n < max_exact, n, large)


def _bias_block_nonzero(variant, delta):
    min_dist = variant * TK + delta * SUB - (SUB - 1)
    far_from = int(np.argmax(_rel_bucket_table(4 * TK) == REL_BUCKETS - 1))
    return min_dist < far_from


def _bias_block_index():
    nblk = TK // SUB
    needed = [(0, d) for d in range(nblk) if _bias_block_nonzero(0, d)]
    needed += [(1, d) for d in range(-(nblk - 1), nblk) if _bias_block_nonzero(1, d)]
    return {key: idx for idx, key in enumerate(needed)}


def _near_bias_blocks(rel_bias):
    assert (_rel_bucket_table(8 * TK)[TK + 1:] == REL_BUCKETS - 1).all()
    heads = rel_bias.shape[1]
    win = 2 * SUB - 1
    dist = np.stack([variant * TK + delta * SUB - (SUB - 1) + np.arange(win)
                     for variant, delta in _bias_block_index()])
    bucket = _rel_bucket_table(TK + TQ)[np.maximum(dist, 0)]
    shifted = (rel_bias.astype(F32) - rel_bias[REL_BUCKETS - 1].astype(F32)) * LOG2E
    w = jnp.where(dist[None] >= 0, jnp.transpose(shifted[bucket], (2, 0, 1)), MASK_VALUE)
    w = jnp.pad(w, ((0, 0), (0, 0), (0, 1)))
    m = jnp.tile(w, (1, 1, SUB))[:, :, :SUB * win].reshape(heads, dist.shape[0], SUB, win)
    return m[:, :, :, SUB - 1:2 * SUB - 1]


def _chain_rows(seq):
    return tuple((c + 1) * SUB if seq == TQ else TK for c in range(TQ // SUB))


def _attn_kernel(q_ref, k_ref, vt_ref, bias_ref, lamp_ref, g_ref, o_ref,
                 qb_ref, m_ref, acc_ref, s_ref, p_ref, *, lambda_init):
    lp = lamp_ref[...]
    lam = (jnp.exp(jnp.sum(lp[0:1] * lp[1:2], axis=-1, keepdims=True))
           - jnp.exp(jnp.sum(lp[2:3] * lp[3:4], axis=-1, keepdims=True)) + lambda_init)
    refs = (q_ref, k_ref, vt_ref, bias_ref, g_ref, o_ref, qb_ref, m_ref, acc_ref, s_ref, p_ref)
    n_tiles = q_ref.shape[0] // TQ
    if n_tiles == 1:
        _attn_q_tile(0, lam, *refs, lambda_init)
    else:
        def q_tile(i, carry):
            _attn_q_tile(i, lam, *refs, lambda_init)
            return carry

        lax.fori_loop(0, n_tiles, q_tile, 0)


def _attn_q_tile(i, lam, q_ref, k_ref, vt_ref, bias_ref, g_ref, o_ref,
                 qb_ref, m_ref, acc_ref, s_ref, p_ref, lambda_init):
    nsub = TQ // SUB
    single = isinstance(i, int)
    row0 = np.concatenate([[0], np.cumsum(_chain_rows(TQ if single else None))]).tolist()
    crow = lambda c, nk: slice(row0[c], row0[c] + nk)
    chains = [(ch, ch // nsub, ch % nsub) for ch in range(HEADS_PER_STEP * nsub)]
    hcols = lambda hh: slice(hh * ATTN_VD, (hh + 1) * ATTN_VD)
    bias_index = _bias_block_index()
    q0 = 0 if single else pl.multiple_of(i * TQ, TQ)
    lane = lax.broadcasted_iota(jnp.int32, (SUB, ATTN_VD), 1)
    for ch, hh, c in chains:
        qc = q_ref[pl.ds(q0 + c * SUB, SUB), hcols(hh)]
        zero = jnp.zeros_like(qc)
        qb_ref[ch, 0:SUB, :] = jnp.where(lane < ATTN_HD, qc, zero)
        qb_ref[ch, SUB:2 * SUB, :] = jnp.where(lane >= ATTN_HD, qc, zero)
        if not single:
            m_ref[ch] = jnp.full((1, 2 * SUB), MASK_VALUE, F32)
            acc_ref[ch] = jnp.zeros((ACC_ROWS, 2 * SUB), F32)

    def step(ks, n_keys, variant):
        tile_max = []
        for ch, hh, c in chains:
            nk = n_keys[c]
            s = lax.dot_general(k_ref[pl.ds(ks, nk), hcols(hh)], qb_ref[ch], (((1,), (1,)), ((), ())),
                                preferred_element_type=F32)
            if variant is not None:
                parts = []
                for bk in range(nk // SUB):
                    blk = s[bk * SUB:(bk + 1) * SUB, :]
                    if (variant, c - bk) in bias_index:
                        b = bias_ref[hh, bias_index[(variant, c - bk)]]
                        blk = blk + jnp.concatenate([b, b], axis=1)
                    parts.append(blk)
                s = jnp.concatenate(parts, axis=0)
            s_ref[hh, crow(c, nk), :] = s
            tile_max.append(jnp.max(s, axis=0, keepdims=True))
        alphas = []
        for ch, hh, c in chains:
            nk = n_keys[c]
            if single:
                m_new = tile_max[ch]
            else:
                m_old = m_ref[ch]
                m_new = jnp.maximum(m_old, tile_max[ch])
                alphas.append(jnp.exp2(m_old - m_new))
                m_ref[ch] = m_new
            p_ref[hh, crow(c, nk), :] = jnp.exp2(s_ref[hh, crow(c, nk), :] - m_new).astype(BF16)
        for ch, hh, c in chains:
            nk = n_keys[c]
            vte = jnp.concatenate([vt_ref[hcols(hh), pl.ds(ks, nk)], jnp.ones((ONES_ROWS, nk), BF16)],
                                  axis=0)
            pv = jnp.dot(vte, p_ref[hh, crow(c, nk), :], preferred_element_type=F32)
            acc_ref[ch] = pv if single else alphas[ch] * acc_ref[ch] + pv

    full = (TK,) * nsub

    def far(j, carry):
        step(pl.multiple_of(j * TK, TK), full, None)
        return carry

    if not single:
        lax.fori_loop(0, jnp.maximum(i - 1, 0), far, 0)

        @pl.when(i >= 1)
        def _():
            step(pl.multiple_of((i - 1) * TK, TK), full, 1)

    step(0 if single else pl.multiple_of(i * TK, TK), tuple((c + 1) * SUB for c in range(nsub)), 0)

    for ch, hh, c in chains:
        acc = acc_ref[ch, 0:ATTN_VD, :]
        l = acc_ref[ch, ATTN_VD:ATTN_VD + 1, :]
        ot = acc[:, :SUB] / l[:, :SUB] - lam * (acc[:, SUB:] / l[:, SUB:])
        o = _rmsnorm(ot.T, g_ref[...]) * (1.0 - lambda_init)
        o_ref[pl.ds(q0 + c * SUB, SUB), hcols(hh)] = o.astype(BF16)


def _attention(q, k, vt, bias_blocks, lam_p, subln_g, batch, seq, layer):
    n = batch * seq
    nchain = HEADS_PER_STEP * (TQ // SUB)
    hw = HEADS_PER_STEP * ATTN_VD
    lambda_init = 0.8 - 0.6 * math.exp(-0.3 * layer)
    return pl.pallas_call(
        functools.partial(_attn_kernel, lambda_init=lambda_init),
        grid=(batch, ATTN_HEADS // HEADS_PER_STEP),
        in_specs=[
            pl.BlockSpec((seq, hw), lambda b, h: (b, h)),
            pl.BlockSpec((seq, hw), lambda b, h: (b, h)),
            pl.BlockSpec((hw, seq), lambda b, h: (h, b)),
            pl.BlockSpec((HEADS_PER_STEP, len(_bias_block_index()), SUB, SUB),
                         lambda b, h: (h, 0, 0, 0)),
            pl.BlockSpec((None, 4, ATTN_HD), lambda b, h: (layer, 0, 0)),
            pl.BlockSpec((None, 1, ATTN_VD), lambda b, h: (layer, 0, 0)),
        ],
        out_specs=pl.BlockSpec((seq, hw), lambda b, h: (b, h)),
        out_shape=jax.ShapeDtypeStruct((n, ATTN_W), BF16),
        scratch_shapes=[
            pltpu.VMEM((nchain, 2 * SUB, ATTN_VD), BF16),
            pltpu.VMEM((nchain, 1, 2 * SUB), F32),
            pltpu.VMEM((nchain, ACC_ROWS, 2 * SUB), F32),
            pltpu.VMEM((HEADS_PER_STEP, sum(_chain_rows(seq)), 2 * SUB), F32),
            pltpu.VMEM((HEADS_PER_STEP, sum(_chain_rows(seq)), 2 * SUB), BF16),
        ],
        compiler_params=pltpu.CompilerParams(
            dimension_semantics=("arbitrary", "arbitrary"),
            vmem_limit_bytes=VMEM_LIMIT),
        name="diff_attn",
    )(q, k, vt, bias_blocks, lam_p, subln_g)


def _mixer_kernel(x_ref, u_ref, uh_ref, pu_ref, ph_ref, o_ref, gate_ref,
                  dww_ref, dwb_ref, lng_ref, lnb_ref, wco_ref, wao_ref,
                  pw_ref, ps_ref, wpo_ref, wout_ref, gpost_ref,
                  out_ref,
                  ext_ref, pext_ref, cs_ref, pool_ref, merged_ref, *, tiles_per_seq):
    tm = TM_MIX
    t_in_seq = lax.rem(pl.program_id(0), tiles_per_seq)
    keep = t_in_seq > 0
    n_slab = CONV_CH // LANES

    def fill(dst_ref, tile_ref, halo_ref):
        for s in range(n_slab):
            cols = slice(s * LANES, (s + 1) * LANES)
            dst_ref[s, 0:HALO, :] = jnp.where(keep, halo_ref[:, cols].astype(F32), 0.0)
            dst_ref[s, HALO:HALO + tm, :] = tile_ref[:, cols].astype(F32)

    fill(ext_ref, u_ref, uh_ref)
    first_off = HALO - (CONV_WIDTH - 1)

    def conv_rows(base):
        conv = []
        for s in range(n_slab):
            cols = slice(s * LANES, (s + 1) * LANES)
            acc = jnp.zeros((CONV_RB, LANES), F32)
            for j in range(CONV_WIDTH):
                acc = acc + ext_ref[s, pl.ds(base + first_off + j, CONV_RB), :] * dww_ref[j:j + 1, cols]
            conv.append(acc + dwb_ref[:, cols])
        mu = sum(jnp.sum(a, axis=-1, keepdims=True) for a in conv) * (1.0 / CONV_CH)
        cen = [a - mu for a in conv]
        var = sum(jnp.sum(a * a, axis=-1, keepdims=True) for a in cen) * (1.0 / CONV_CH)
        inv = lax.rsqrt(var + EPS)
        for s in range(n_slab):
            cols = slice(s * LANES, (s + 1) * LANES)
            y = cen[s] * inv * lng_ref[:, cols] + lnb_ref[:, cols]
            cs_ref[pl.ds(base, CONV_RB), cols] = (y * _sigmoid(y)).astype(BF16)

    fill(pext_ref, pu_ref, ph_ref)
    pos = t_in_seq * tm + lax.broadcasted_iota(jnp.int32, (tm, POOL_GC), 0)
    for g, w in enumerate(POOL_WINDOWS):
        cols = slice(g * POOL_GC, (g + 1) * POOL_GC)
        e = pext_ref[g, HALO:HALO + tm, :]
        tot = e
        for j in range(1, w):
            tot = tot + pext_ref[g, HALO - j:HALO - j + tm, :]
        cnt = jnp.minimum(pos + 1, w).astype(F32)
        pooled = tot / cnt - e
        yg = jnp.dot(pooled.astype(BF16), pw_ref[g], preferred_element_type=F32)
        pool_ref[:, cols] = (yg * ps_ref[:, cols]).astype(BF16)

    nc = 256
    for r0 in range(0, tm, MIX_ROWS):
        rows = slice(r0, r0 + MIX_ROWS)
        for base in range(r0, r0 + MIX_ROWS, CONV_RB):
            conv_rows(base)
        for c in range(0, D_MODEL, nc):
            cols = slice(c, c + nc)
            y_conv = jnp.dot(cs_ref[rows, :], wco_ref[:, cols], preferred_element_type=F32)
            y_attn = jnp.dot(o_ref[rows, :], wao_ref[:, cols], preferred_element_type=F32)
            y_pool = jnp.dot(pool_ref[rows, :], wpo_ref[:, cols], preferred_element_type=F32)
            g0 = gate_ref[rows, c:c + nc].astype(F32)
            g1 = gate_ref[rows, D_MODEL + c:D_MODEL + c + nc].astype(F32)
            g2 = gate_ref[rows, 2 * D_MODEL + c:2 * D_MODEL + c + nc].astype(F32)
            merged_ref[rows, cols] = (g0 * y_conv + g1 * y_attn + g2 * y_pool).astype(BF16)
        mix = jnp.dot(merged_ref[rows, :], wout_ref[...], preferred_element_type=F32)
        out_ref[rows, :] = x_ref[rows, :] + _rmsnorm(mix, gpost_ref[...])


def _mixer(x2, u, pu, o, gates, dww, dwb, lng, lnb, wco, wao, pw, ps, wpo, wout, gpost, seq, layer):
    n = x2.shape[0]
    tm = TM_MIX
    tiles_per_seq = seq // tm
    row = lambda width: pl.BlockSpec((tm, width), lambda i: (i, 0))
    halo = lambda width: pl.BlockSpec(
        (HALO, width), lambda i: (jnp.maximum(i * (tm // HALO) - 1, 0), 0))
    return pl.pallas_call(
        functools.partial(_mixer_kernel, tiles_per_seq=tiles_per_seq),
        grid=(n // tm,),
        in_specs=[
            row(D_MODEL), row(CONV_CH), halo(CONV_CH), row(POOL_CH), halo(POOL_CH),
            row(ATTN_W), row(N_BRANCH * D_MODEL),
            _resident((CONV_WIDTH, CONV_CH), layer), _resident((1, CONV_CH), layer),
            _resident((1, CONV_CH), layer), _resident((1, CONV_CH), layer),
            _resident((CONV_CH, D_MODEL), layer), _resident((ATTN_W, D_MODEL), layer),
            _resident((len(POOL_WINDOWS), POOL_GC, POOL_GC), layer), _resident((1, POOL_CH), layer),
            _resident((POOL_CH, D_MODEL), layer), _resident((D_MODEL, D_MODEL), layer),
            _resident((1, D_MODEL), layer),
        ],
        out_specs=row(D_MODEL),
        out_shape=jax.ShapeDtypeStruct((n, D_MODEL), F32),
        scratch_shapes=[
            pltpu.VMEM((CONV_CH // LANES, tm + HALO, LANES), F32),
            pltpu.VMEM((POOL_CH // LANES, tm + HALO, LANES), F32),
            pltpu.VMEM((tm, CONV_CH), BF16),
            pltpu.VMEM((tm, POOL_CH), BF16),
            pltpu.VMEM((tm, D_MODEL), BF16),
        ],
        compiler_params=pltpu.CompilerParams(
            dimension_semantics=("arbitrary",), vmem_limit_bytes=VMEM_LIMIT),
        name="mixer",
    )(x2, u, u, pu, pu, o, gates, dww, dwb, lng, lnb, wco, wao, pw, ps, wpo, wout, gpost)


def _mlp_kernel(x_ref, p_ref, gpre_ref, w1_ref, w2_ref, gpost_ref, wpp_ref, wpg_ref,
                out_ref, hid_ref):
    x = x_ref[...]
    h = _rmsnorm(x, gpre_ref[...]).astype(BF16)
    fc = 512
    for c in range(0, D_FF, fc):
        a = jnp.dot(h, w1_ref[:, c:c + fc], preferred_element_type=F32)
        a = jnp.maximum(a, 0.0)
        hid_ref[:, c:c + fc] = (a * a).astype(BF16)
    f = jnp.dot(hid_ref[...], w2_ref[...], preferred_element_type=F32)
    x = x + _rmsnorm(f, gpost_ref[...])
    gate = _sigmoid(jnp.dot(x.astype(BF16), wpg_ref[...], preferred_element_type=F32))
    pe = jnp.dot(p_ref[...].astype(BF16), wpp_ref[...], preferred_element_type=F32)
    out_ref[...] = x + gate * pe


def _mlp(x2, p3, gpre, w1, w2, gpost, wpp, wpg, layer):
    n = x2.shape[0]
    tm = TM_MLP
    row = lambda width: pl.BlockSpec((tm, width), lambda i: (i, 0))
    return pl.pallas_call(
        _mlp_kernel,
        grid=(n // tm,),
        in_specs=[
            row(D_MODEL), pl.BlockSpec((None, tm, PLE_DIM), lambda i: (layer, i, 0)),
            _resident((1, D_MODEL), layer),
            _resident((D_MODEL, D_FF), layer), _resident((D_FF, D_MODEL), layer),
            _resident((1, D_MODEL), layer),
            _resident((PLE_DIM, D_MODEL), layer), _resident((D_MODEL, D_MODEL), layer),
        ],
        out_specs=row(D_MODEL),
        out_shape=jax.ShapeDtypeStruct((n, D_MODEL), F32),
        scratch_shapes=[pltpu.VMEM((tm, D_FF), BF16)],
        compiler_params=pltpu.CompilerParams(
            dimension_semantics=("arbitrary",), vmem_limit_bytes=VMEM_LIMIT),
        name="mlp_ple",
    )(x2, p3, gpre, w1, w2, gpost, wpp, wpg)


def kernel(x, p, rel_bias, g_pre_mix, w_in, conv_dw_w, conv_dw_b, conv_ln_g, conv_ln_b, w_conv_out, lam_p, subln_g, w_attn_out, pool_w, pool_scale, w_pool_out, w_out, g_post_mix, g_pre_mlp, w_mlp_in, w_mlp_out, g_post_mlp, w_ple_proj, w_ple_gate):
    batch, seq, d = x.shape
    depth = w_in.shape[0]
    assert d == D_MODEL and seq % TM_MIX == 0 and seq % TQ == 0 and TQ == TK
    n = batch * seq
    x2 = x.reshape(n, d)
    bias_blocks = _near_bias_blocks(rel_bias)
    rows = lambda a: a.reshape(depth, 1, -1)
    bf = lambda a: a.astype(BF16)
    w_in_bf = bf(w_in)
    wvt_bf = bf(jnp.swapaxes(lax.optimization_barrier(w_in[:, :, V_COL:V_COL + ATTN_W]), 1, 2))
    p3 = p.reshape(depth, n, PLE_DIM)
    mixer_params = (conv_dw_w.reshape(depth, CONV_WIDTH, CONV_CH), rows(conv_dw_b),
                    rows(conv_ln_g), rows(conv_ln_b), bf(w_conv_out), bf(w_attn_out),
                    bf(pool_w), rows(pool_scale), bf(w_pool_out), bf(w_out), rows(g_post_mix))
    mlp_params = (rows(g_pre_mlp), bf(w_mlp_in), bf(w_mlp_out), rows(g_post_mlp),
                  bf(w_ple_proj), bf(w_ple_gate))
    g_pre, subln = rows(g_pre_mix), rows(subln_g)
    for layer in range(depth):
        u, q, k, vt, pu, gates = _proj_in(x2, g_pre, w_in_bf, wvt_bf, layer)
        o = _attention(q, k, vt, bias_blocks, lam_p, subln, batch, seq, layer)
        x2 = _mixer(x2, u, pu, o, gates, *mixer_params, seq, layer)
        x2 = _mlp(x2, p3, *mlp_params, layer)
    return x2.reshape(batch, seq, d)
```

```python
import functools
import math

import jax
import jax.numpy as jnp
import numpy as np
from jax import lax
from jax.experimental import pallas as pl
from jax.experimental.pallas import tpu as pltpu

F32 = jnp.float32
BF16 = jnp.bfloat16

D_MODEL = 1024
PLE_DIM = 256
CONV_CH = 512
CONV_WIDTH = 31
ATTN_HEADS = 4
ATTN_HD = 64
ATTN_VD = 2 * ATTN_HD
ATTN_W = ATTN_HEADS * ATTN_VD
POOL_CH = 512
POOL_WINDOWS = (2, 4, 8, 16)
POOL_GC = POOL_CH // len(POOL_WINDOWS)
N_BRANCH = 3
D_FF = 4 * D_MODEL
REL_BUCKETS = 32
REL_MAX_DIST = 128
EPS = 1e-6
MASK_VALUE = -1e30

TM_PROJ = 1024
PROJ_ROWS = 256
TM_MIX = 1024
TM_MLP = 1024
TQ = 2048
TK = 2048
SUB = 128
HEADS_PER_STEP = 1
ONES_ROWS = 16
ACC_ROWS = ATTN_VD + ONES_ROWS
LOG2E = math.log2(math.e)
HALO = 32
CONV_RB = 64
MIX_ROWS = 256
LANES = 128
VMEM_LIMIT = 60000 * 1024


def _sigmoid(x):
    return 0.5 * jnp.tanh(0.5 * x) + 0.5


def _rmsnorm(x, g):
    return x * lax.rsqrt(jnp.mean(x * x, axis=-1, keepdims=True) + EPS) * g


def _resident(shape, layer):
    index = (layer,) + (0,) * len(shape)
    return pl.BlockSpec((None,) + tuple(shape), lambda *_: index, pipeline_mode=pl.Buffered(1))


V_COL = 2 * CONV_CH + 2 * ATTN_W


def _proj_in_kernel(x_ref, g_ref, w_ref, wvt_ref, u_ref, q_ref, k_ref, vt_ref, pu_ref, gate_ref):
    for r0 in range(0, TM_PROJ, PROJ_ROWS):
        rows = slice(r0, r0 + PROJ_ROWS)
        h = _rmsnorm(x_ref[rows, :], g_ref[...]).astype(BF16)

        def proj(c0, width):
            return jnp.dot(h, w_ref[:, c0:c0 + width], preferred_element_type=F32)

        ca = proj(0, CONV_CH)
        cb = proj(CONV_CH, CONV_CH)
        u_ref[rows, :] = (ca * _sigmoid(cb)).astype(BF16)
        off = 2 * CONV_CH
        q_ref[rows, :] = (proj(off, ATTN_W) * (ATTN_HD ** -0.5 * LOG2E)).astype(BF16)
        k_ref[rows, :] = proj(off + ATTN_W, ATTN_W).astype(BF16)
        vt_ref[:, rows] = lax.dot_general(wvt_ref[...], h, (((1,), (1,)), ((), ())),
                                          preferred_element_type=F32).astype(BF16)
        pu_ref[rows, :] = proj(V_COL + ATTN_W, POOL_CH).astype(BF16)
        off = V_COL + ATTN_W + POOL_CH
        for c in range(0, N_BRANCH * D_MODEL, 512):
            gate_ref[rows, c:c + 512] = _sigmoid(proj(off + c, 512)).astype(BF16)


def _proj_in(x2, g, w_bf, wvt_bf, layer):
    n = x2.shape[0]
    d_in = w_bf.shape[-1]
    row = lambda width: pl.BlockSpec((TM_PROJ, width), lambda i: (i, 0))
    widths = (CONV_CH, ATTN_W, ATTN_W, None, POOL_CH, N_BRANCH * D_MODEL)
    out_specs = [row(w) if w else pl.BlockSpec((ATTN_W, TM_PROJ), lambda i: (0, i)) for w in widths]
    out_shape = [jax.ShapeDtypeStruct((n, w) if w else (ATTN_W, n), BF16) for w in widths]
    return pl.pallas_call(
        _proj_in_kernel,
        grid=(n // TM_PROJ,),
        in_specs=[row(D_MODEL), _resident((1, D_MODEL), layer), _resident((D_MODEL, d_in), layer),
                  _resident((ATTN_W, D_MODEL), layer)],
        out_specs=out_specs,
        out_shape=out_shape,
        compiler_params=pltpu.CompilerParams(
            dimension_semantics=("arbitrary",), vmem_limit_bytes=VMEM_LIMIT),
        name="proj_in",
    )(x2, g, w_bf, wvt_bf)


def _rel_bucket_table(n_max):
    n = np.arange(n_max, dtype=np.int32)
    max_exact = REL_BUCKETS // 2
    nf = np.maximum(n, 1).astype(np.float32)
    large = max_exact + (np.log(nf / max_exact) / np.float32(math.log(REL_MAX_DIST / max_exact))
                         * (REL_BUCKETS - max_exact)).astype(np.int32)
    large = np.minimum(large, REL_BUCKETS - 1)
    return np.where(n < max_exact, n, large)


def _bias_block_nonzero(variant, delta):
    min_dist = variant * TK + delta * SUB - (SUB - 1)
    far_from = int(np.argmax(_rel_bucket_table(4 * TK) == REL_BUCKETS - 1))
    return min_dist < far_from


def _bias_block_index():
    nblk = TK // SUB
    needed = [(0, d) for d in range(nblk) if _bias_block_nonzero(0, d)]
    needed += [(1, d) for d in range(-(nblk - 1), nblk) if _bias_block_nonzero(1, d)]
    return {key: idx for idx, key in enumerate(needed)}


def _near_bias_blocks(rel_bias):
    assert (_rel_bucket_table(8 * TK)[TK + 1:] == REL_BUCKETS - 1).all()
    heads = rel_bias.shape[1]
    win = 2 * SUB - 1
    dist = np.stack([variant * TK + delta * SUB - (SUB - 1) + np.arange(win)
                     for variant, delta in _bias_block_index()])
    bucket = _rel_bucket_table(TK + TQ)[np.maximum(dist, 0)]
    shifted = (rel_bias.astype(F32) - rel_bias[REL_BUCKETS - 1].astype(F32)) * LOG2E
    w = jnp.where(dist[None] >= 0, jnp.transpose(shifted[bucket], (2, 0, 1)), MASK_VALUE)
    w = jnp.pad(w, ((0, 0), (0, 0), (0, 1)))
    m = jnp.tile(w, (1, 1, SUB))[:, :, :SUB * win].reshape(heads, dist.shape[0], SUB, win)
    return m[:, :, :, SUB - 1:2 * SUB - 1]


def _chain_rows(seq):
    return tuple((c + 1) * SUB if seq == TQ else TK for c in range(TQ // SUB))


def _attn_kernel(q_ref, k_ref, vt_ref, bias_ref, lamp_ref, g_ref, o_ref,
                 qb_ref, m_ref, acc_ref, s_ref, p_ref, *, lambda_init):
    lp = lamp_ref[...]
    lam = (jnp.exp(jnp.sum(lp[0:1] * lp[1:2], axis=-1, keepdims=True))
           - jnp.exp(jnp.sum(lp[2:3] * lp[3:4], axis=-1, keepdims=True)) + lambda_init)
    refs = (q_ref, k_ref, vt_ref, bias_ref, g_ref, o_ref, qb_ref, m_ref, acc_ref, s_ref, p_ref)
    n_tiles = q_ref.shape[0] // TQ
    if n_tiles == 1:
        _attn_q_tile(0, lam, *refs, lambda_init)
    else:
        def q_tile(i, carry):
            _attn_q_tile(i, lam, *refs, lambda_init)
            return carry

        lax.fori_loop(0, n_tiles, q_tile, 0)


def _attn_q_tile(i, lam, q_ref, k_ref, vt_ref, bias_ref, g_ref, o_ref,
                 qb_ref, m_ref, acc_ref, s_ref, p_ref, lambda_init):
    nsub = TQ // SUB
    single = isinstance(i, int)
    row0 = np.concatenate([[0], np.cumsum(_chain_rows(TQ if single else None))]).tolist()
    crow = lambda c, nk: slice(row0[c], row0[c] + nk)
    chains = [(ch, ch // nsub, ch % nsub) for ch in range(HEADS_PER_STEP * nsub)]
    hcols = lambda hh: slice(hh * ATTN_VD, (hh + 1) * ATTN_VD)
    bias_index = _bias_block_index()
    q0 = 0 if single else pl.multiple_of(i * TQ, TQ)
    lane = lax.broadcasted_iota(jnp.int32, (SUB, ATTN_VD), 1)
    for ch, hh, c in chains:
        qc = q_ref[pl.ds(q0 + c * SUB, SUB), hcols(hh)]
        zero = jnp.zeros_like(qc)
        qb_ref[ch, 0:SUB, :] = jnp.where(lane < ATTN_HD, qc, zero)
        qb_ref[ch, SUB:2 * SUB, :] = jnp.where(lane >= ATTN_HD, qc, zero)
        if not single:
            m_ref[ch] = jnp.full((1, 2 * SUB), MASK_VALUE, F32)
            acc_ref[ch] = jnp.zeros((ACC_ROWS, 2 * SUB), F32)

    def step(ks, n_keys, variant):
        tile_max = []
        for ch, hh, c in chains:
            nk = n_keys[c]
            s = lax.dot_general(k_ref[pl.ds(ks, nk), hcols(hh)], qb_ref[ch], (((1,), (1,)), ((), ())),
                                preferred_element_type=F32)
            if variant is not None:
                parts = []
                for bk in range(nk // SUB):
                    blk = s[bk * SUB:(bk + 1) * SUB, :]
                    if (variant, c - bk) in bias_index:
                        b = bias_ref[hh, bias_index[(variant, c - bk)]]
                        b = jnp.concatenate([b, b], axis=1)
                        blk = blk + b
                        if variant == 0 and c == bk:
                            blk = jnp.where(b == MASK_VALUE, MASK_VALUE, blk)
                    parts.append(blk)
                s = jnp.concatenate(parts, axis=0)
            s_ref[hh, crow(c, nk), :] = s
            tile_max.append(jnp.max(s, axis=0, keepdims=True))
        alphas = []
        for ch, hh, c in chains:
            nk = n_keys[c]
            if single:
                m_new = tile_max[ch]
            else:
                m_old = m_ref[ch]
                m_new = jnp.maximum(m_old, tile_max[ch])
                alphas.append(jnp.exp2(m_old - m_new))
                m_ref[ch] = m_new
            p_ref[hh, crow(c, nk), :] = jnp.exp2(s_ref[hh, crow(c, nk), :] - m_new).astype(BF16)
        for ch, hh, c in chains:
            nk = n_keys[c]
            vte = jnp.concatenate([vt_ref[hcols(hh), pl.ds(ks, nk)], jnp.ones((ONES_ROWS, nk), BF16)],
                                  axis=0)
            pv = jnp.dot(vte, p_ref[hh, crow(c, nk), :], preferred_element_type=F32)
            acc_ref[ch] = pv if single else alphas[ch] * acc_ref[ch] + pv

    full = (TK,) * nsub

    def far(j, carry):
        step(pl.multiple_of(j * TK, TK), full, None)
        return carry

    if not single:
        lax.fori_loop(0, jnp.maximum(i - 1, 0), far, 0)

        @pl.when(i >= 1)
        def _():
            step(pl.multiple_of((i - 1) * TK, TK), full, 1)

    step(0 if single else pl.multiple_of(i * TK, TK), tuple((c + 1) * SUB for c in range(nsub)), 0)

    for ch, hh, c in chains:
        acc = acc_ref[ch, 0:ATTN_VD, :]
        l = acc_ref[ch, ATTN_VD:ATTN_VD + 1, :]
        ot = acc[:, :SUB] / l[:, :SUB] - lam * (acc[:, SUB:] / l[:, SUB:])
        o = _rmsnorm(ot.T, g_ref[...]) * (1.0 - lambda_init)
        o_ref[pl.ds(q0 + c * SUB, SUB), hcols(hh)] = o.astype(BF16)


def _attention(q, k, vt, bias_blocks, lam_p, subln_g, batch, seq, layer):
    n = batch * seq
    nchain = HEADS_PER_STEP * (TQ // SUB)
    hw = HEADS_PER_STEP * ATTN_VD
    lambda_init = 0.8 - 0.6 * math.exp(-0.3 * layer)
    return pl.pallas_call(
        functools.partial(_attn_kernel, lambda_init=lambda_init),
        grid=(batch, ATTN_HEADS // HEADS_PER_STEP),
        in_specs=[
            pl.BlockSpec((seq, hw), lambda b, h: (b, h)),
            pl.BlockSpec((seq, hw), lambda b, h: (b, h)),
            pl.BlockSpec((hw, seq), lambda b, h: (h, b)),
            pl.BlockSpec((HEADS_PER_STEP, len(_bias_block_index()), SUB, SUB),
                         lambda b, h: (h, 0, 0, 0)),
            pl.BlockSpec((None, 4, ATTN_HD), lambda b, h: (layer, 0, 0)),
            pl.BlockSpec((None, 1, ATTN_VD), lambda b, h: (layer, 0, 0)),
        ],
        out_specs=pl.BlockSpec((seq, hw), lambda b, h: (b, h)),
        out_shape=jax.ShapeDtypeStruct((n, ATTN_W), BF16),
        scratch_shapes=[
            pltpu.VMEM((nchain, 2 * SUB, ATTN_VD), BF16),
            pltpu.VMEM((nchain, 1, 2 * SUB), F32),
            pltpu.VMEM((nchain, ACC_ROWS, 2 * SUB), F32),
            pltpu.VMEM((HEADS_PER_STEP, sum(_chain_rows(seq)), 2 * SUB), F32),
            pltpu.VMEM((HEADS_PER_STEP, sum(_chain_rows(seq)), 2 * SUB), BF16),
        ],
        compiler_params=pltpu.CompilerParams(
            dimension_semantics=("arbitrary", "arbitrary"),
            vmem_limit_bytes=VMEM_LIMIT),
        name="diff_attn",
    )(q, k, vt, bias_blocks, lam_p, subln_g)


def _mixer_kernel(x_ref, u_ref, uh_ref, pu_ref, ph_ref, o_ref, gate_ref,
                  dww_ref, dwb_ref, lng_ref, lnb_ref, wco_ref, wao_ref,
                  pw_ref, ps_ref, wpo_ref, wout_ref, gpost_ref,
                  out_ref,
                  ext_ref, pext_ref, cs_ref, pool_ref, merged_ref, *, tiles_per_seq):
    tm = TM_MIX
    t_in_seq = lax.rem(pl.program_id(0), tiles_per_seq)
    keep = t_in_seq > 0
    n_slab = CONV_CH // LANES

    def fill(dst_ref, tile_ref, halo_ref):
        for s in range(n_slab):
            cols = slice(s * LANES, (s + 1) * LANES)
            dst_ref[s, 0:HALO, :] = jnp.where(keep, halo_ref[:, cols].astype(F32), 0.0)
            dst_ref[s, HALO:HALO + tm, :] = tile_ref[:, cols].astype(F32)

    fill(ext_ref, u_ref, uh_ref)
    first_off = HALO - (CONV_WIDTH - 1)

    def conv_rows(base):
        conv = []
        for s in range(n_slab):
            cols = slice(s * LANES, (s + 1) * LANES)
            acc = jnp.zeros((CONV_RB, LANES), F32)
            for j in range(CONV_WIDTH):
                acc = acc + ext_ref[s, pl.ds(base + first_off + j, CONV_RB), :] * dww_ref[j:j + 1, cols]
            conv.append(acc + dwb_ref[:, cols])
        mu = sum(jnp.sum(a, axis=-1, keepdims=True) for a in conv) * (1.0 / CONV_CH)
        cen = [a - mu for a in conv]
        var = sum(jnp.sum(a * a, axis=-1, keepdims=True) for a in cen) * (1.0 / CONV_CH)
        inv = lax.rsqrt(var + EPS)
        for s in range(n_slab):
            cols = slice(s * LANES, (s + 1) * LANES)
            y = cen[s] * inv * lng_ref[:, cols] + lnb_ref[:, cols]
            cs_ref[pl.ds(base, CONV_RB), cols] = (y * _sigmoid(y)).astype(BF16)

    fill(pext_ref, pu_ref, ph_ref)
    pos = t_in_seq * tm + lax.broadcasted_iota(jnp.int32, (tm, POOL_GC), 0)
    for g, w in enumerate(POOL_WINDOWS):
        cols = slice(g * POOL_GC, (g + 1) * POOL_GC)
        e = pext_ref[g, HALO:HALO + tm, :]
        tot = e
        for j in range(1, w):
            tot = tot + pext_ref[g, HALO - j:HALO - j + tm, :]
        cnt = jnp.minimum(pos + 1, w).astype(F32)
        pooled = tot / cnt - e
        yg = jnp.dot(pooled.astype(BF16), pw_ref[g], preferred_element_type=F32)
        pool_ref[:, cols] = (yg * ps_ref[:, cols]).astype(BF16)

    nc = 256
    for r0 in range(0, tm, MIX_ROWS):
        rows = slice(r0, r0 + MIX_ROWS)
        for base in range(r0, r0 + MIX_ROWS, CONV_RB):
            conv_rows(base)
        for c in range(0, D_MODEL, nc):
            cols = slice(c, c + nc)
            y_conv = jnp.dot(cs_ref[rows, :], wco_ref[:, cols], preferred_element_type=F32)
            y_attn = jnp.dot(o_ref[rows, :], wao_ref[:, cols], preferred_element_type=F32)
            y_pool = jnp.dot(pool_ref[rows, :], wpo_ref[:, cols], preferred_element_type=F32)
            g0 = gate_ref[rows, c:c + nc].astype(F32)
            g1 = gate_ref[rows, D_MODEL + c:D_MODEL + c + nc].astype(F32)
            g2 = gate_ref[rows, 2 * D_MODEL + c:2 * D_MODEL + c + nc].astype(F32)
            merged_ref[rows, cols] = (g0 * y_conv + g1 * y_attn + g2 * y_pool).astype(BF16)
        mix = jnp.dot(merged_ref[rows, :], wout_ref[...], preferred_element_type=F32)
        out_ref[rows, :] = x_ref[rows, :] + _rmsnorm(mix, gpost_ref[...])


def _mixer(x2, u, pu, o, gates, dww, dwb, lng, lnb, wco, wao, pw, ps, wpo, wout, gpost, seq, layer):
    n = x2.shape[0]
    tm = TM_MIX
    tiles_per_seq = seq // tm
    row = lambda width: pl.BlockSpec((tm, width), lambda i: (i, 0))
    halo = lambda width: pl.BlockSpec(
        (HALO, width), lambda i: (jnp.maximum(i * (tm // HALO) - 1, 0), 0))
    return pl.pallas_call(
        functools.partial(_mixer_kernel, tiles_per_seq=tiles_per_seq),
        grid=(n // tm,),
        in_specs=[
            row(D_MODEL), row(CONV_CH), halo(CONV_CH), row(POOL_CH), halo(POOL_CH),
            row(ATTN_W), row(N_BRANCH * D_MODEL),
            _resident((CONV_WIDTH, CONV_CH), layer), _resident((1, CONV_CH), layer),
            _resident((1, CONV_CH), layer), _resident((1, CONV_CH), layer),
            _resident((CONV_CH, D_MODEL), layer), _resident((ATTN_W, D_MODEL), layer),
            _resident((len(POOL_WINDOWS), POOL_GC, POOL_GC), layer), _resident((1, POOL_CH), layer),
            _resident((POOL_CH, D_MODEL), layer), _resident((D_MODEL, D_MODEL), layer),
            _resident((1, D_MODEL), layer),
        ],
        out_specs=row(D_MODEL),
        out_shape=jax.ShapeDtypeStruct((n, D_MODEL), F32),
        scratch_shapes=[
            pltpu.VMEM((CONV_CH // LANES, tm + HALO, LANES), F32),
            pltpu.VMEM((POOL_CH // LANES, tm + HALO, LANES), F32),
            pltpu.VMEM((tm, CONV_CH), BF16),
            pltpu.VMEM((tm, POOL_CH), BF16),
            pltpu.VMEM((tm, D_MODEL), BF16),
        ],
        compiler_params=pltpu.CompilerParams(
            dimension_semantics=("arbitrary",), vmem_limit_bytes=VMEM_LIMIT),
        name="mixer",
    )(x2, u, u, pu, pu, o, gates, dww, dwb, lng, lnb, wco, wao, pw, ps, wpo, wout, gpost)


def _mlp_kernel(x_ref, p_ref, gpre_ref, w1_ref, w2_ref, gpost_ref, wpp_ref, wpg_ref,
                out_ref, hid_ref):
    x = x_ref[...]
    h = _rmsnorm(x, gpre_ref[...]).astype(BF16)
    fc = 512
    for c in range(0, D_FF, fc):
        a = jnp.dot(h, w1_ref[:, c:c + fc], preferred_element_type=F32)
        a = jnp.maximum(a, 0.0)
        hid_ref[:, c:c + fc] = (a * a).astype(BF16)
    f = jnp.dot(hid_ref[...], w2_ref[...], preferred_element_type=F32)
    x = x + _rmsnorm(f, gpost_ref[...])
    gate = _sigmoid(jnp.dot(x.astype(BF16), wpg_ref[...], preferred_element_type=F32))
    pe = jnp.dot(p_ref[...].astype(BF16), wpp_ref[...], preferred_element_type=F32)
    out_ref[...] = x + gate * pe


def _mlp(x2, p3, gpre, w1, w2, gpost, wpp, wpg, layer):
    n = x2.shape[0]
    tm = TM_MLP
    row = lambda width: pl.BlockSpec((tm, width), lambda i: (i, 0))
    return pl.pallas_call(
        _mlp_kernel,
        grid=(n // tm,),
        in_specs=[
            row(D_MODEL), pl.BlockSpec((None, tm, PLE_DIM), lambda i: (layer, i, 0)),
            _resident((1, D_MODEL), layer),
            _resident((D_MODEL, D_FF), layer), _resident((D_FF, D_MODEL), layer),
            _resident((1, D_MODEL), layer),
            _resident((PLE_DIM, D_MODEL), layer), _resident((D_MODEL, D_MODEL), layer),
        ],
        out_specs=row(D_MODEL),
        out_shape=jax.ShapeDtypeStruct((n, D_MODEL), F32),
        scratch_shapes=[pltpu.VMEM((tm, D_FF), BF16)],
        compiler_params=pltpu.CompilerParams(
            dimension_semantics=("arbitrary",), vmem_limit_bytes=VMEM_LIMIT),
        name="mlp_ple",
    )(x2, p3, gpre, w1, w2, gpost, wpp, wpg)


def kernel(x, p, rel_bias, g_pre_mix, w_in, conv_dw_w, conv_dw_b, conv_ln_g, conv_ln_b, w_conv_out, lam_p, subln_g, w_attn_out, pool_w, pool_scale, w_pool_out, w_out, g_post_mix, g_pre_mlp, w_mlp_in, w_mlp_out, g_post_mlp, w_ple_proj, w_ple_gate):
    batch, seq, d = x.shape
    depth = w_in.shape[0]
    assert d == D_MODEL and seq % TM_MIX == 0 and seq % TQ == 0 and TQ == TK
    n = batch * seq
    x2 = x.reshape(n, d)
    bias_blocks = _near_bias_blocks(rel_bias)
    rows = lambda a: a.reshape(depth, 1, -1)
    bf = lambda a: a.astype(BF16)
    w_in_bf = bf(w_in)
    wvt_bf = bf(jnp.swapaxes(lax.optimization_barrier(w_in[:, :, V_COL:V_COL + ATTN_W]), 1, 2))
    p3 = p.reshape(depth, n, PLE_DIM)
    mixer_params = (conv_dw_w.reshape(depth, CONV_WIDTH, CONV_CH), rows(conv_dw_b),
                    rows(conv_ln_g), rows(conv_ln_b), bf(w_conv_out), bf(w_attn_out),
                    bf(pool_w), rows(pool_scale), bf(w_pool_out), bf(w_out), rows(g_post_mix))
    mlp_params = (rows(g_pre_mlp), bf(w_mlp_in), bf(w_mlp_out), rows(g_post_mlp),
                  bf(w_ple_proj), bf(w_ple_gate))
    g_pre, subln = rows(g_pre_mix), rows(subln_g)
    for layer in range(depth):
        u, q, k, vt, pu, gates = _proj_in(x2, g_pre, w_in_bf, wvt_bf, layer)
        o = _attention(q, k, vt, bias_blocks, lam_p, subln, batch, seq, layer)
        x2 = _mixer(x2, u, pu, o, gates, *mixer_params, seq, layer)
        x2 = _mlp(x2, p3, *mlp_params, layer)
    return x2.reshape(batch, seq, d)
```

```python
import functools
import math

import jax
import jax.numpy as jnp
import numpy as np
from jax import lax
from jax.experimental import pallas as pl
from jax.experimental.pallas import tpu as pltpu

F32 = jnp.float32
BF16 = jnp.bfloat16

D_MODEL = 1024
PLE_DIM = 256
CONV_CH = 512
CONV_WIDTH = 31
ATTN_HEADS = 4
ATTN_HD = 64
ATTN_VD = 2 * ATTN_HD
ATTN_W = ATTN_HEADS * ATTN_VD
POOL_CH = 512
POOL_WINDOWS = (2, 4, 8, 16)
POOL_GC = POOL_CH // len(POOL_WINDOWS)
N_BRANCH = 3
D_FF = 4 * D_MODEL
REL_BUCKETS = 32
REL_MAX_DIST = 128
EPS = 1e-6
MASK_VALUE = -1e30

TM_PROJ = 1024
PROJ_ROWS = 256
TM_MIX = 1024
TM_MLP = 1024
TQ = 2048
TK = 2048
SUB = 128
HEADS_PER_STEP = 1
ONES_ROWS = 16
ACC_ROWS = ATTN_VD + ONES_ROWS
LOG2E = math.log2(math.e)
HALO = 32
CONV_RB = 64
MIX_ROWS = 256
LANES = 128
VMEM_LIMIT = 60000 * 1024


def _sigmoid(x):
    return 0.5 * jnp.tanh(0.5 * x) + 0.5


def _rmsnorm(x, g):
    return x * lax.rsqrt(jnp.mean(x * x, axis=-1, keepdims=True) + EPS) * g


def _resident(shape, layer):
    index = (layer,) + (0,) * len(shape)
    return pl.BlockSpec((None,) + tuple(shape), lambda *_: index, pipeline_mode=pl.Buffered(1))


V_COL = 2 * CONV_CH + 2 * ATTN_W


def _proj_in_kernel(x_ref, g_ref, w_ref, wvt_ref, u_ref, q_ref, k_ref, vt_ref, pu_ref, gate_ref):
    for r0 in range(0, TM_PROJ, PROJ_ROWS):
        rows = slice(r0, r0 + PROJ_ROWS)
        h = _rmsnorm(x_ref[rows, :], g_ref[...]).astype(BF16)

        def proj(c0, width):
            return jnp.dot(h, w_ref[:, c0:c0 + width], preferred_element_type=F32)

        ca = proj(0, CONV_CH)
        cb = proj(CONV_CH, CONV_CH)
        u_ref[rows, :] = ca * _sigmoid(cb)
        off = 2 * CONV_CH
        q_ref[rows, :] = (proj(off, ATTN_W) * (ATTN_HD ** -0.5 * LOG2E)).astype(BF16)
        k_ref[rows, :] = proj(off + ATTN_W, ATTN_W).astype(BF16)
        vt_ref[:, rows] = lax.dot_general(wvt_ref[...], h, (((1,), (1,)), ((), ())),
                                          preferred_element_type=F32).astype(BF16)
        pu_ref[rows, :] = proj(V_COL + ATTN_W, POOL_CH)
        off = V_COL + ATTN_W + POOL_CH
        for c in range(0, N_BRANCH * D_MODEL, 512):
            gate_ref[rows, c:c + 512] = _sigmoid(proj(off + c, 512)).astype(BF16)


def _proj_in(x2, g, w_bf, wvt_bf, layer):
    n = x2.shape[0]
    d_in = w_bf.shape[-1]
    row = lambda width: pl.BlockSpec((TM_PROJ, width), lambda i: (i, 0))
    widths = (CONV_CH, ATTN_W, ATTN_W, None, POOL_CH, N_BRANCH * D_MODEL)
    out_specs = [row(w) if w else pl.BlockSpec((ATTN_W, TM_PROJ), lambda i: (0, i)) for w in widths]
    dtypes = (F32, BF16, BF16, BF16, F32, BF16)
    out_shape = [jax.ShapeDtypeStruct((n, w) if w else (ATTN_W, n), dt) for w, dt in zip(widths, dtypes)]
    return pl.pallas_call(
        _proj_in_kernel,
        grid=(n // TM_PROJ,),
        in_specs=[row(D_MODEL), _resident((1, D_MODEL), layer), _resident((D_MODEL, d_in), layer),
                  _resident((ATTN_W, D_MODEL), layer)],
        out_specs=out_specs,
        out_shape=out_shape,
        compiler_params=pltpu.CompilerParams(
            dimension_semantics=("arbitrary",), vmem_limit_bytes=VMEM_LIMIT),
        name="proj_in",
    )(x2, g, w_bf, wvt_bf)


def _rel_bucket_table(n_max):
    n = np.arange(n_max, dtype=np.int32)
    max_exact = REL_BUCKETS // 2
    nf = np.maximum(n, 1).astype(np.float32)
    large = max_exact + (np.log(nf / max_exact) / np.float32(math.log(REL_MAX_DIST / max_exact))
                         * (REL_BUCKETS - max_exact)).astype(np.int32)
    large = np.minimum(large, REL_BUCKETS - 1)
    return np.where(n < max_exact, n, large)


def _bias_block_nonzero(variant, delta):
    min_dist = variant * TK + delta * SUB - (SUB - 1)
    far_from = int(np.argmax(_rel_bucket_table(4 * TK) == REL_BUCKETS - 1))
    return min_dist < far_from


def _bias_block_index():
    nblk = TK // SUB
    needed = [(0, d) for d in range(nblk) if _bias_block_nonzero(0, d)]
    needed += [(1, d) for d in range(-(nblk - 1), nblk) if _bias_block_nonzero(1, d)]
    return {key: idx for idx, key in enumerate(needed)}


def _near_bias_blocks(rel_bias):
    assert (_rel_bucket_table(8 * TK)[TK + 1:] == REL_BUCKETS - 1).all()
    heads = rel_bias.shape[1]
    win = 2 * SUB - 1
    dist = np.stack([variant * TK + delta * SUB - (SUB - 1) + np.arange(win)
                     for variant, delta in _bias_block_index()])
    bucket = _rel_bucket_table(TK + TQ)[np.maximum(dist, 0)]
    shifted = (rel_bias.astype(F32) - rel_bias[REL_BUCKETS - 1].astype(F32)) * LOG2E
    w = jnp.where(dist[None] >= 0, jnp.transpose(shifted[bucket], (2, 0, 1)), MASK_VALUE)
    w = jnp.pad(w, ((0, 0), (0, 0), (0, 1)))
    m = jnp.tile(w, (1, 1, SUB))[:, :, :SUB * win].reshape(heads, dist.shape[0], SUB, win)
    return m[:, :, :, SUB - 1:2 * SUB - 1]


def _chain_rows(seq):
    return tuple((c + 1) * SUB if seq == TQ else TK for c in range(TQ // SUB))


def _attn_kernel(q_ref, k_ref, vt_ref, bias_ref, lamp_ref, g_ref, o_ref,
                 qb_ref, m_ref, acc_ref, s_ref, p_ref, *, lambda_init):
    lp = lamp_ref[...]
    lam = (jnp.exp(jnp.sum(lp[0:1] * lp[1:2], axis=-1, keepdims=True))
           - jnp.exp(jnp.sum(lp[2:3] * lp[3:4], axis=-1, keepdims=True)) + lambda_init)
    refs = (q_ref, k_ref, vt_ref, bias_ref, g_ref, o_ref, qb_ref, m_ref, acc_ref, s_ref, p_ref)
    n_tiles = q_ref.shape[0] // TQ
    if n_tiles == 1:
        _attn_q_tile(0, lam, *refs, lambda_init)
    else:
        def q_tile(i, carry):
            _attn_q_tile(i, lam, *refs, lambda_init)
            return carry

        lax.fori_loop(0, n_tiles, q_tile, 0)


def _attn_q_tile(i, lam, q_ref, k_ref, vt_ref, bias_ref, g_ref, o_ref,
                 qb_ref, m_ref, acc_ref, s_ref, p_ref, lambda_init):
    nsub = TQ // SUB
    single = isinstance(i, int)
    row0 = np.concatenate([[0], np.cumsum(_chain_rows(TQ if single else None))]).tolist()
    crow = lambda c, nk: slice(row0[c], row0[c] + nk)
    chains = [(ch, ch // nsub, ch % nsub) for ch in range(HEADS_PER_STEP * nsub)]
    hcols = lambda hh: slice(hh * ATTN_VD, (hh + 1) * ATTN_VD)
    bias_index = _bias_block_index()
    q0 = 0 if single else pl.multiple_of(i * TQ, TQ)
    lane = lax.broadcasted_iota(jnp.int32, (SUB, ATTN_VD), 1)
    for ch, hh, c in chains:
        qc = q_ref[pl.ds(q0 + c * SUB, SUB), hcols(hh)]
        zero = jnp.zeros_like(qc)
        qb_ref[ch, 0:SUB, :] = jnp.where(lane < ATTN_HD, qc, zero)
        qb_ref[ch, SUB:2 * SUB, :] = jnp.where(lane >= ATTN_HD, qc, zero)
        if not single:
            m_ref[ch] = jnp.full((1, 2 * SUB), MASK_VALUE, F32)
            acc_ref[ch] = jnp.zeros((ACC_ROWS, 2 * SUB), F32)

    def step(ks, n_keys, variant):
        tile_max = []
        for ch, hh, c in chains:
            nk = n_keys[c]
            s = lax.dot_general(k_ref[pl.ds(ks, nk), hcols(hh)], qb_ref[ch], (((1,), (1,)), ((), ())),
                                preferred_element_type=F32)
            if variant is not None:
                parts = []
                for bk in range(nk // SUB):
                    blk = s[bk * SUB:(bk + 1) * SUB, :]
                    if (variant, c - bk) in bias_index:
                        b = bias_ref[hh, bias_index[(variant, c - bk)]]
                        b = jnp.concatenate([b, b], axis=1)
                        blk = blk + b
                        if variant == 0 and c == bk:
                            blk = jnp.where(b == MASK_VALUE, MASK_VALUE, blk)
                    parts.append(blk)
                s = jnp.concatenate(parts, axis=0)
            s_ref[hh, crow(c, nk), :] = s
            tile_max.append(jnp.max(s, axis=0, keepdims=True))
        alphas = []
        for ch, hh, c in chains:
            nk = n_keys[c]
            if single:
                m_new = tile_max[ch]
            else:
                m_old = m_ref[ch]
                m_new = jnp.maximum(m_old, tile_max[ch])
                alphas.append(jnp.exp2(m_old - m_new))
                m_ref[ch] = m_new
            p_ref[hh, crow(c, nk), :] = jnp.exp2(s_ref[hh, crow(c, nk), :] - m_new).astype(BF16)
        for ch, hh, c in chains:
            nk = n_keys[c]
            vte = jnp.concatenate([vt_ref[hcols(hh), pl.ds(ks, nk)], jnp.ones((ONES_ROWS, nk), BF16)],
                                  axis=0)
            pv = jnp.dot(vte, p_ref[hh, crow(c, nk), :], preferred_element_type=F32)
            acc_ref[ch] = pv if single else alphas[ch] * acc_ref[ch] + pv

    full = (TK,) * nsub

    def far(j, carry):
        step(pl.multiple_of(j * TK, TK), full, None)
        return carry

    if not single:
        lax.fori_loop(0, jnp.maximum(i - 1, 0), far, 0)

        @pl.when(i >= 1)
        def _():
            step(pl.multiple_of((i - 1) * TK, TK), full, 1)

    step(0 if single else pl.multiple_of(i * TK, TK), tuple((c + 1) * SUB for c in range(nsub)), 0)

    for ch, hh, c in chains:
        acc = acc_ref[ch, 0:ATTN_VD, :]
        l = acc_ref[ch, ATTN_VD:ATTN_VD + 1, :]
        ot = acc[:, :SUB] / l[:, :SUB] - lam * (acc[:, SUB:] / l[:, SUB:])
        o = _rmsnorm(ot.T, g_ref[...]) * (1.0 - lambda_init)
        o_ref[pl.ds(q0 + c * SUB, SUB), hcols(hh)] = o.astype(BF16)


def _attention(q, k, vt, bias_blocks, lam_p, subln_g, batch, seq, layer):
    n = batch * seq
    nchain = HEADS_PER_STEP * (TQ // SUB)
    hw = HEADS_PER_STEP * ATTN_VD
    lambda_init = 0.8 - 0.6 * math.exp(-0.3 * layer)
    return pl.pallas_call(
        functools.partial(_attn_kernel, lambda_init=lambda_init),
        grid=(batch, ATTN_HEADS // HEADS_PER_STEP),
        in_specs=[
            pl.BlockSpec((seq, hw), lambda b, h: (b, h)),
            pl.BlockSpec((seq, hw), lambda b, h: (b, h)),
            pl.BlockSpec((hw, seq), lambda b, h: (h, b)),
            pl.BlockSpec((HEADS_PER_STEP, len(_bias_block_index()), SUB, SUB),
                         lambda b, h: (h, 0, 0, 0)),
            pl.BlockSpec((None, 4, ATTN_HD), lambda b, h: (layer, 0, 0)),
            pl.BlockSpec((None, 1, ATTN_VD), lambda b, h: (layer, 0, 0)),
        ],
        out_specs=pl.BlockSpec((seq, hw), lambda b, h: (b, h)),
        out_shape=jax.ShapeDtypeStruct((n, ATTN_W), BF16),
        scratch_shapes=[
            pltpu.VMEM((nchain, 2 * SUB, ATTN_VD), BF16),
            pltpu.VMEM((nchain, 1, 2 * SUB), F32),
            pltpu.VMEM((nchain, ACC_ROWS, 2 * SUB), F32),
            pltpu.VMEM((HEADS_PER_STEP, sum(_chain_rows(seq)), 2 * SUB), F32),
            pltpu.VMEM((HEADS_PER_STEP, sum(_chain_rows(seq)), 2 * SUB), BF16),
        ],
        compiler_params=pltpu.CompilerParams(
            dimension_semantics=("arbitrary", "arbitrary"),
            vmem_limit_bytes=VMEM_LIMIT),
        name="diff_attn",
    )(q, k, vt, bias_blocks, lam_p, subln_g)


def _mixer_kernel(x_ref, u_ref, uh_ref, pu_ref, ph_ref, o_ref, gate_ref,
                  dww_ref, dwb_ref, lng_ref, lnb_ref, wco_ref, wao_ref,
                  pw_ref, ps_ref, wpo_ref, wout_ref, gpost_ref,
                  out_ref,
                  ext_ref, pext_ref, cs_ref, pool_ref, merged_ref, *, tiles_per_seq):
    tm = TM_MIX
    t_in_seq = lax.rem(pl.program_id(0), tiles_per_seq)
    keep = t_in_seq > 0
    n_slab = CONV_CH // LANES

    def fill(dst_ref, tile_ref, halo_ref):
        for s in range(n_slab):
            cols = slice(s * LANES, (s + 1) * LANES)
            dst_ref[s, 0:HALO, :] = jnp.where(keep, halo_ref[:, cols].astype(F32), 0.0)
            dst_ref[s, HALO:HALO + tm, :] = tile_ref[:, cols].astype(F32)

    fill(ext_ref, u_ref, uh_ref)
    first_off = HALO - (CONV_WIDTH - 1)

    def conv_rows(base):
        conv = []
        for s in range(n_slab):
            cols = slice(s * LANES, (s + 1) * LANES)
            acc = jnp.zeros((CONV_RB, LANES), F32)
            for j in range(CONV_WIDTH):
                acc = acc + ext_ref[s, pl.ds(base + first_off + j, CONV_RB), :] * dww_ref[j:j + 1, cols]
            conv.append(acc + dwb_ref[:, cols])
        mu = sum(jnp.sum(a, axis=-1, keepdims=True) for a in conv) * (1.0 / CONV_CH)
        cen = [a - mu for a in conv]
        var = sum(jnp.sum(a * a, axis=-1, keepdims=True) for a in cen) * (1.0 / CONV_CH)
        inv = lax.rsqrt(var + EPS)
        for s in range(n_slab):
            cols = slice(s * LANES, (s + 1) * LANES)
            y = cen[s] * inv * lng_ref[:, cols] + lnb_ref[:, cols]
            cs_ref[pl.ds(base, CONV_RB), cols] = (y * _sigmoid(y)).astype(BF16)

    fill(pext_ref, pu_ref, ph_ref)
    pos = t_in_seq * tm + lax.broadcasted_iota(jnp.int32, (tm, POOL_GC), 0)
    for g, w in enumerate(POOL_WINDOWS):
        cols = slice(g * POOL_GC, (g + 1) * POOL_GC)
        e = pext_ref[g, HALO:HALO + tm, :]
        tot = e
        for j in range(1, w):
            tot = tot + pext_ref[g, HALO - j:HALO - j + tm, :]
        cnt = jnp.minimum(pos + 1, w).astype(F32)
        pooled = tot / cnt - e
        yg = jnp.dot(pooled.astype(BF16), pw_ref[g], preferred_element_type=F32)
        pool_ref[:, cols] = (yg * ps_ref[:, cols]).astype(BF16)

    nc = 256
    for r0 in range(0, tm, MIX_ROWS):
        rows = slice(r0, r0 + MIX_ROWS)
        for base in range(r0, r0 + MIX_ROWS, CONV_RB):
            conv_rows(base)
        for c in range(0, D_MODEL, nc):
            cols = slice(c, c + nc)
            y_conv = jnp.dot(cs_ref[rows, :], wco_ref[:, cols], preferred_element_type=F32)
            y_attn = jnp.dot(o_ref[rows, :], wao_ref[:, cols], preferred_element_type=F32)
            y_pool = jnp.dot(pool_ref[rows, :], wpo_ref[:, cols], preferred_element_type=F32)
            g0 = gate_ref[rows, c:c + nc].astype(F32)
            g1 = gate_ref[rows, D_MODEL + c:D_MODEL + c + nc].astype(F32)
            g2 = gate_ref[rows, 2 * D_MODEL + c:2 * D_MODEL + c + nc].astype(F32)
            merged_ref[rows, cols] = (g0 * y_conv + g1 * y_attn + g2 * y_pool).astype(BF16)
        mix = jnp.dot(merged_ref[rows, :], wout_ref[...], preferred_element_type=F32)
        out_ref[rows, :] = x_ref[rows, :] + _rmsnorm(mix, gpost_ref[...])


def _mixer(x2, u, pu, o, gates, dww, dwb, lng, lnb, wco, wao, pw, ps, wpo, wout, gpost, seq, layer):
    n = x2.shape[0]
    tm = TM_MIX
    tiles_per_seq = seq // tm
    row = lambda width: pl.BlockSpec((tm, width), lambda i: (i, 0))
    halo = lambda width: pl.BlockSpec(
        (HALO, width), lambda i: (jnp.maximum(i * (tm // HALO) - 1, 0), 0))
    return pl.pallas_call(
        functools.partial(_mixer_kernel, tiles_per_seq=tiles_per_seq),
        grid=(n // tm,),
        in_specs=[
            row(D_MODEL), row(CONV_CH), halo(CONV_CH), row(POOL_CH), halo(POOL_CH),
            row(ATTN_W), row(N_BRANCH * D_MODEL),
            _resident((CONV_WIDTH, CONV_CH), layer), _resident((1, CONV_CH), layer),
            _resident((1, CONV_CH), layer), _resident((1, CONV_CH), layer),
            _resident((CONV_CH, D_MODEL), layer), _resident((ATTN_W, D_MODEL), layer),
            _resident((len(POOL_WINDOWS), POOL_GC, POOL_GC), layer), _resident((1, POOL_CH), layer),
            _resident((POOL_CH, D_MODEL), layer), _resident((D_MODEL, D_MODEL), layer),
            _resident((1, D_MODEL), layer),
        ],
        out_specs=row(D_MODEL),
        out_shape=jax.ShapeDtypeStruct((n, D_MODEL), F32),
        scratch_shapes=[
            pltpu.VMEM((CONV_CH // LANES, tm + HALO, LANES), F32),
            pltpu.VMEM((POOL_CH // LANES, tm + HALO, LANES), F32),
            pltpu.VMEM((tm, CONV_CH), BF16),
            pltpu.VMEM((tm, POOL_CH), BF16),
            pltpu.VMEM((tm, D_MODEL), BF16),
        ],
        compiler_params=pltpu.CompilerParams(
            dimension_semantics=("arbitrary",), vmem_limit_bytes=VMEM_LIMIT),
        name="mixer",
    )(x2, u, u, pu, pu, o, gates, dww, dwb, lng, lnb, wco, wao, pw, ps, wpo, wout, gpost)


def _mlp_kernel(x_ref, p_ref, gpre_ref, w1_ref, w2_ref, gpost_ref, wpp_ref, wpg_ref,
                out_ref, hid_ref):
    x = x_ref[...]
    h = _rmsnorm(x, gpre_ref[...]).astype(BF16)
    fc = 512
    for c in range(0, D_FF, fc):
        a = jnp.dot(h, w1_ref[:, c:c + fc], preferred_element_type=F32)
        a = jnp.maximum(a, 0.0)
        hid_ref[:, c:c + fc] = (a * a).astype(BF16)
    f = jnp.dot(hid_ref[...], w2_ref[...], preferred_element_type=F32)
    x = x + _rmsnorm(f, gpost_ref[...])
    gate = _sigmoid(jnp.dot(x.astype(BF16), wpg_ref[...], preferred_element_type=F32))
    pe = jnp.dot(p_ref[...].astype(BF16), wpp_ref[...], preferred_element_type=F32)
    out_ref[...] = x + gate * pe


def _mlp(x2, p3, gpre, w1, w2, gpost, wpp, wpg, layer):
    n = x2.shape[0]
    tm = TM_MLP
    row = lambda width: pl.BlockSpec((tm, width), lambda i: (i, 0))
    return pl.pallas_call(
        _mlp_kernel,
        grid=(n // tm,),
        in_specs=[
            row(D_MODEL), pl.BlockSpec((None, tm, PLE_DIM), lambda i: (layer, i, 0)),
            _resident((1, D_MODEL), layer),
            _resident((D_MODEL, D_FF), layer), _resident((D_FF, D_MODEL), layer),
            _resident((1, D_MODEL), layer),
            _resident((PLE_DIM, D_MODEL), layer), _resident((D_MODEL, D_MODEL), layer),
        ],
        out_specs=row(D_MODEL),
        out_shape=jax.ShapeDtypeStruct((n, D_MODEL), F32),
        scratch_shapes=[pltpu.VMEM((tm, D_FF), BF16)],
        compiler_params=pltpu.CompilerParams(
            dimension_semantics=("arbitrary",), vmem_limit_bytes=VMEM_LIMIT),
        name="mlp_ple",
    )(x2, p3, gpre, w1, w2, gpost, wpp, wpg)


def kernel(x, p, rel_bias, g_pre_mix, w_in, conv_dw_w, conv_dw_b, conv_ln_g, conv_ln_b, w_conv_out, lam_p, subln_g, w_attn_out, pool_w, pool_scale, w_pool_out, w_out, g_post_mix, g_pre_mlp, w_mlp_in, w_mlp_out, g_post_mlp, w_ple_proj, w_ple_gate):
    batch, seq, d = x.shape
    depth = w_in.shape[0]
    assert d == D_MODEL and seq % TM_MIX == 0 and seq % TQ == 0 and TQ == TK
    n = batch * seq
    x2 = x.reshape(n, d)
    bias_blocks = _near_bias_blocks(rel_bias)
    rows = lambda a: a.reshape(depth, 1, -1)
    bf = lambda a: a.astype(BF16)
    w_in_bf = bf(w_in)
    wvt_bf = bf(jnp.swapaxes(lax.optimization_barrier(w_in[:, :, V_COL:V_COL + ATTN_W]), 1, 2))
    p3 = p.reshape(depth, n, PLE_DIM)
    mixer_params = (conv_dw_w.reshape(depth, CONV_WIDTH, CONV_CH), rows(conv_dw_b),
                    rows(conv_ln_g), rows(conv_ln_b), bf(w_conv_out), bf(w_attn_out),
                    bf(pool_w), rows(pool_scale), bf(w_pool_out), bf(w_out), rows(g_post_mix))
    mlp_params = (rows(g_pre_mlp), bf(w_mlp_in), bf(w_mlp_out), rows(g_post_mlp),
                  bf(w_ple_proj), bf(w_ple_gate))
    g_pre, subln = rows(g_pre_mix), rows(subln_g)
    for layer in range(depth):
        u, q, k, vt, pu, gates = _proj_in(x2, g_pre, w_in_bf, wvt_bf, layer)
        o = _attention(q, k, vt, bias_blocks, lam_p, subln, batch, seq, layer)
        x2 = _mixer(x2, u, pu, o, gates, *mixer_params, seq, layer)
        x2 = _mlp(x2, p3, *mlp_params, layer)
    return x2.reshape(batch, seq, d)
```

```python
import functools
import math

import jax
import jax.numpy as jnp
import numpy as np
from jax import lax
from jax.experimental import pallas as pl
from jax.experimental.pallas import tpu as pltpu

F32 = jnp.float32
BF16 = jnp.bfloat16

D_MODEL = 1024
PLE_DIM = 256
CONV_CH = 512
CONV_WIDTH = 31
ATTN_HEADS = 4
ATTN_HD = 64
ATTN_VD = 2 * ATTN_HD
ATTN_W = ATTN_HEADS * ATTN_VD
POOL_CH = 512
POOL_WINDOWS = (2, 4, 8, 16)
POOL_GC = POOL_CH // len(POOL_WINDOWS)
N_BRANCH = 3
D_FF = 4 * D_MODEL
REL_BUCKETS = 32
REL_MAX_DIST = 128
EPS = 1e-6
MASK_VALUE = -1e30

TM_PROJ = 1024
PROJ_ROWS = 256
TM_MIX = 1024
TM_MLP = 1024
TQ = 2048
TK = 2048
SUB = 128
HEADS_PER_STEP = 1
ONES_ROWS = 16
ACC_ROWS = ATTN_VD + ONES_ROWS
LOG2E = math.log2(math.e)
HALO = 32
CONV_RB = 64
MIX_ROWS = 256
POOL_ROWS = 256
LANES = 128
VMEM_LIMIT = 60000 * 1024


def _sigmoid(x):
    return 0.5 * jnp.tanh(0.5 * x) + 0.5


def _rmsnorm(x, g):
    return x * lax.rsqrt(jnp.mean(x * x, axis=-1, keepdims=True) + EPS) * g


def _resident(shape, layer):
    index = (layer,) + (0,) * len(shape)
    return pl.BlockSpec((None,) + tuple(shape), lambda *_: index, pipeline_mode=pl.Buffered(1))


V_COL = 2 * CONV_CH + 2 * ATTN_W


def _proj_in_kernel(x_ref, g_ref, w_ref, wvt_ref, u_ref, q_ref, k_ref, vt_ref, pu_ref, gate_ref):
    for r0 in range(0, TM_PROJ, PROJ_ROWS):
        rows = slice(r0, r0 + PROJ_ROWS)
        h = _rmsnorm(x_ref[rows, :], g_ref[...]).astype(BF16)

        def proj(c0, width):
            return jnp.dot(h, w_ref[:, c0:c0 + width], preferred_element_type=F32)

        ca = proj(0, CONV_CH)
        cb = proj(CONV_CH, CONV_CH)
        u_ref[rows, :] = ca * _sigmoid(cb)
        off = 2 * CONV_CH
        q_ref[rows, :] = (proj(off, ATTN_W) * (ATTN_HD ** -0.5 * LOG2E)).astype(BF16)
        k_ref[rows, :] = proj(off + ATTN_W, ATTN_W).astype(BF16)
        vt_ref[:, rows] = lax.dot_general(wvt_ref[...], h, (((1,), (1,)), ((), ())),
                                          preferred_element_type=F32).astype(BF16)
        pu_ref[rows, :] = proj(V_COL + ATTN_W, POOL_CH).astype(BF16)
        off = V_COL + ATTN_W + POOL_CH
        for c in range(0, N_BRANCH * D_MODEL, 512):
            gate_ref[rows, c:c + 512] = _sigmoid(proj(off + c, 512)).astype(BF16)


def _proj_in(x2, g, w_bf, wvt_bf, layer):
    n = x2.shape[0]
    d_in = w_bf.shape[-1]
    row = lambda width: pl.BlockSpec((TM_PROJ, width), lambda i: (i, 0))
    widths = (CONV_CH, ATTN_W, ATTN_W, None, POOL_CH, N_BRANCH * D_MODEL)
    out_specs = [row(w) if w else pl.BlockSpec((ATTN_W, TM_PROJ), lambda i: (0, i)) for w in widths]
    dtypes = (F32, BF16, BF16, BF16, BF16, BF16)
    out_shape = [jax.ShapeDtypeStruct((n, w) if w else (ATTN_W, n), dt) for w, dt in zip(widths, dtypes)]
    return pl.pallas_call(
        _proj_in_kernel,
        grid=(n // TM_PROJ,),
        in_specs=[row(D_MODEL), _resident((1, D_MODEL), layer), _resident((D_MODEL, d_in), layer),
                  _resident((ATTN_W, D_MODEL), layer)],
        out_specs=out_specs,
        out_shape=out_shape,
        compiler_params=pltpu.CompilerParams(
            dimension_semantics=("arbitrary",), vmem_limit_bytes=VMEM_LIMIT),
        name="proj_in",
    )(x2, g, w_bf, wvt_bf)


def _rel_bucket_table(n_max):
    n = np.arange(n_max, dtype=np.int32)
    max_exact = REL_BUCKETS // 2
    nf = np.maximum(n, 1).astype(np.float32)
    large = max_exact + (np.log(nf / max_exact) / np.float32(math.log(REL_MAX_DIST / max_exact))
                         * (REL_BUCKETS - max_exact)).astype(np.int32)
    large = np.minimum(large, REL_BUCKETS - 1)
    return np.where(n < max_exact, n, large)


def _bias_block_nonzero(variant, delta):
    min_dist = variant * TK + delta * SUB - (SUB - 1)
    far_from = int(np.argmax(_rel_bucket_table(4 * TK) == REL_BUCKETS - 1))
    return min_dist < far_from


def _bias_block_index():
    nblk = TK // SUB
    needed = [(0, d) for d in range(nblk) if _bias_block_nonzero(0, d)]
    needed += [(1, d) for d in range(-(nblk - 1), nblk) if _bias_block_nonzero(1, d)]
    return {key: idx for idx, key in enumerate(needed)}


def _near_bias_blocks(rel_bias):
    assert (_rel_bucket_table(8 * TK)[TK + 1:] == REL_BUCKETS - 1).all()
    heads = rel_bias.shape[1]
    win = 2 * SUB - 1
    dist = np.stack([variant * TK + delta * SUB - (SUB - 1) + np.arange(win)
                     for variant, delta in _bias_block_index()])
    bucket = _rel_bucket_table(TK + TQ)[np.maximum(dist, 0)]
    shifted = (rel_bias.astype(F32) - rel_bias[REL_BUCKETS - 1].astype(F32)) * LOG2E
    w = jnp.where(dist[None] >= 0, jnp.transpose(shifted[bucket], (2, 0, 1)), MASK_VALUE)
    w = jnp.pad(w, ((0, 0), (0, 0), (0, 1)))
    m = jnp.tile(w, (1, 1, SUB))[:, :, :SUB * win].reshape(heads, dist.shape[0], SUB, win)
    return m[:, :, :, SUB - 1:2 * SUB - 1]


def _chain_rows(seq):
    return tuple((c + 1) * SUB if seq == TQ else TK for c in range(TQ // SUB))


def _attn_kernel(q_ref, k_ref, vt_ref, bias_ref, lamp_ref, g_ref, o_ref,
                 qb_ref, m_ref, acc_ref, s_ref, p_ref, *, lambda_init):
    lp = lamp_ref[...]
    lam = (jnp.exp(jnp.sum(lp[0:1] * lp[1:2], axis=-1, keepdims=True))
           - jnp.exp(jnp.sum(lp[2:3] * lp[3:4], axis=-1, keepdims=True)) + lambda_init)
    refs = (q_ref, k_ref, vt_ref, bias_ref, g_ref, o_ref, qb_ref, m_ref, acc_ref, s_ref, p_ref)
    n_tiles = q_ref.shape[0] // TQ
    if n_tiles == 1:
        _attn_q_tile(0, lam, *refs, lambda_init)
    else:
        def q_tile(i, carry):
            _attn_q_tile(i, lam, *refs, lambda_init)
            return carry

        lax.fori_loop(0, n_tiles, q_tile, 0)


def _attn_q_tile(i, lam, q_ref, k_ref, vt_ref, bias_ref, g_ref, o_ref,
                 qb_ref, m_ref, acc_ref, s_ref, p_ref, lambda_init):
    nsub = TQ // SUB
    single = isinstance(i, int)
    row0 = np.concatenate([[0], np.cumsum(_chain_rows(TQ if single else None))]).tolist()
    crow = lambda c, nk: slice(row0[c], row0[c] + nk)
    chains = [(ch, ch // nsub, ch % nsub) for ch in range(HEADS_PER_STEP * nsub)]
    hcols = lambda hh: slice(hh * ATTN_VD, (hh + 1) * ATTN_VD)
    bias_index = _bias_block_index()
    q0 = 0 if single else pl.multiple_of(i * TQ, TQ)
    lane = lax.broadcasted_iota(jnp.int32, (SUB, ATTN_VD), 1)
    for ch, hh, c in chains:
        qc = q_ref[pl.ds(q0 + c * SUB, SUB), hcols(hh)]
        zero = jnp.zeros_like(qc)
        qb_ref[ch, 0:SUB, :] = jnp.where(lane < ATTN_HD, qc, zero)
        qb_ref[ch, SUB:2 * SUB, :] = jnp.where(lane >= ATTN_HD, qc, zero)
        if not single:
            m_ref[ch] = jnp.full((1, 2 * SUB), MASK_VALUE, F32)
            acc_ref[ch] = jnp.zeros((ACC_ROWS, 2 * SUB), F32)

    def step(ks, n_keys, variant):
        tile_max = []
        for ch, hh, c in chains:
            nk = n_keys[c]
            s = lax.dot_general(k_ref[pl.ds(ks, nk), hcols(hh)], qb_ref[ch], (((1,), (1,)), ((), ())),
                                preferred_element_type=F32)
            if variant is not None:
                parts = []
                for bk in range(nk // SUB):
                    blk = s[bk * SUB:(bk + 1) * SUB, :]
                    if (variant, c - bk) in bias_index:
                        b = bias_ref[hh, bias_index[(variant, c - bk)]]
                        b = jnp.concatenate([b, b], axis=1)
                        blk = blk + b
                        if variant == 0 and c == bk:
                            blk = jnp.where(b == MASK_VALUE, MASK_VALUE, blk)
                    parts.append(blk)
                s = jnp.concatenate(parts, axis=0)
            s_ref[hh, crow(c, nk), :] = s
            tile_max.append(jnp.max(s, axis=0, keepdims=True))
        alphas = []
        for ch, hh, c in chains:
            nk = n_keys[c]
            if single:
                m_new = tile_max[ch]
            else:
                m_old = m_ref[ch]
                m_new = jnp.maximum(m_old, tile_max[ch])
                alphas.append(jnp.exp2(m_old - m_new))
                m_ref[ch] = m_new
            p_ref[hh, crow(c, nk), :] = jnp.exp2(s_ref[hh, crow(c, nk), :] - m_new).astype(BF16)
        for ch, hh, c in chains:
            nk = n_keys[c]
            vte = jnp.concatenate([vt_ref[hcols(hh), pl.ds(ks, nk)], jnp.ones((ONES_ROWS, nk), BF16)],
                                  axis=0)
            pv = jnp.dot(vte, p_ref[hh, crow(c, nk), :], preferred_element_type=F32)
            acc_ref[ch] = pv if single else alphas[ch] * acc_ref[ch] + pv

    full = (TK,) * nsub

    def far(j, carry):
        step(pl.multiple_of(j * TK, TK), full, None)
        return carry

    if not single:
        lax.fori_loop(0, jnp.maximum(i - 1, 0), far, 0)

        @pl.when(i >= 1)
        def _():
            step(pl.multiple_of((i - 1) * TK, TK), full, 1)

    step(0 if single else pl.multiple_of(i * TK, TK), tuple((c + 1) * SUB for c in range(nsub)), 0)

    for ch, hh, c in chains:
        acc = acc_ref[ch, 0:ATTN_VD, :]
        l = acc_ref[ch, ATTN_VD:ATTN_VD + 1, :]
        ot = acc[:, :SUB] / l[:, :SUB] - lam * (acc[:, SUB:] / l[:, SUB:])
        o = _rmsnorm(ot.T, g_ref[...]) * (1.0 - lambda_init)
        o_ref[pl.ds(q0 + c * SUB, SUB), hcols(hh)] = o.astype(BF16)


def _attention(q, k, vt, bias_blocks, lam_p, subln_g, batch, seq, layer):
    n = batch * seq
    nchain = HEADS_PER_STEP * (TQ // SUB)
    hw = HEADS_PER_STEP * ATTN_VD
    lambda_init = 0.8 - 0.6 * math.exp(-0.3 * layer)
    return pl.pallas_call(
        functools.partial(_attn_kernel, lambda_init=lambda_init),
        grid=(batch, ATTN_HEADS // HEADS_PER_STEP),
        in_specs=[
            pl.BlockSpec((seq, hw), lambda b, h: (b, h)),
            pl.BlockSpec((seq, hw), lambda b, h: (b, h)),
            pl.BlockSpec((hw, seq), lambda b, h: (h, b)),
            pl.BlockSpec((HEADS_PER_STEP, len(_bias_block_index()), SUB, SUB),
                         lambda b, h: (h, 0, 0, 0)),
            pl.BlockSpec((None, 4, ATTN_HD), lambda b, h: (layer, 0, 0)),
            pl.BlockSpec((None, 1, ATTN_VD), lambda b, h: (layer, 0, 0)),
        ],
        out_specs=pl.BlockSpec((seq, hw), lambda b, h: (b, h)),
        out_shape=jax.ShapeDtypeStruct((n, ATTN_W), BF16),
        scratch_shapes=[
            pltpu.VMEM((nchain, 2 * SUB, ATTN_VD), BF16),
            pltpu.VMEM((nchain, 1, 2 * SUB), F32),
            pltpu.VMEM((nchain, ACC_ROWS, 2 * SUB), F32),
            pltpu.VMEM((HEADS_PER_STEP, sum(_chain_rows(seq)), 2 * SUB), F32),
            pltpu.VMEM((HEADS_PER_STEP, sum(_chain_rows(seq)), 2 * SUB), BF16),
        ],
        compiler_params=pltpu.CompilerParams(
            dimension_semantics=("arbitrary", "arbitrary"),
            vmem_limit_bytes=VMEM_LIMIT),
        name="diff_attn",
    )(q, k, vt, bias_blocks, lam_p, subln_g)


def _pool_bands():
    t = np.arange(POOL_ROWS)[:, None]
    s = np.arange(POOL_ROWS + HALO)[None, :] - HALO
    bands = np.zeros((2, len(POOL_WINDOWS), POOL_ROWS, POOL_ROWS + HALO), np.float32)
    for v in range(2):
        for g, w in enumerate(POOL_WINDOWS):
            cnt = np.minimum(t + 1, w) if v else w
            bands[v, g] = ((s <= t) & (s > t - w)) / cnt - (s == t)
    return jnp.asarray(bands, BF16)


def _mixer_kernel(x_ref, u_ref, uh_ref, pu_ref, ph_ref, o_ref, gate_ref, band_ref,
                  dww_ref, dwb_ref, lng_ref, lnb_ref, wco_ref, wao_ref,
                  pw_ref, ps_ref, wpo_ref, wout_ref, gpost_ref,
                  out_ref,
                  ext_ref, pext_ref, cs_ref, pool_ref, merged_ref, *, tiles_per_seq):
    tm = TM_MIX
    t_in_seq = lax.rem(pl.program_id(0), tiles_per_seq)
    keep = t_in_seq > 0
    n_slab = CONV_CH // LANES

    def fill(dst_ref, tile_ref, halo_ref):
        for s in range(n_slab):
            cols = slice(s * LANES, (s + 1) * LANES)
            dst_ref[s, 0:HALO, :] = jnp.where(keep, halo_ref[:, cols].astype(F32), 0.0)
            dst_ref[s, HALO:HALO + tm, :] = tile_ref[:, cols].astype(F32)

    fill(ext_ref, u_ref, uh_ref)
    first_off = HALO - (CONV_WIDTH - 1)

    def conv_rows(base):
        conv = []
        for s in range(n_slab):
            cols = slice(s * LANES, (s + 1) * LANES)
            acc = jnp.zeros((CONV_RB, LANES), F32)
            for j in range(CONV_WIDTH):
                acc = acc + ext_ref[s, pl.ds(base + first_off + j, CONV_RB), :] * dww_ref[j:j + 1, cols]
            conv.append(acc + dwb_ref[:, cols])
        mu = sum(jnp.sum(a, axis=-1, keepdims=True) for a in conv) * (1.0 / CONV_CH)
        cen = [a - mu for a in conv]
        var = sum(jnp.sum(a * a, axis=-1, keepdims=True) for a in cen) * (1.0 / CONV_CH)
        inv = lax.rsqrt(var + EPS)
        for s in range(n_slab):
            cols = slice(s * LANES, (s + 1) * LANES)
            y = cen[s] * inv * lng_ref[:, cols] + lnb_ref[:, cols]
            cs_ref[pl.ds(base, CONV_RB), cols] = (y * _sigmoid(y)).astype(BF16)

    pext_ref[0:HALO, :] = jnp.where(keep, ph_ref[...], jnp.zeros_like(ph_ref[...]))
    pext_ref[HALO:HALO + tm, :] = pu_ref[...]
    for r0 in range(0, tm, POOL_ROWS):
        variant = (t_in_seq == 0).astype(jnp.int32) if r0 == 0 else 0
        for g in range(len(POOL_WINDOWS)):
            cols = slice(g * POOL_GC, (g + 1) * POOL_GC)
            pooled = jnp.dot(band_ref[variant, g], pext_ref[r0:r0 + POOL_ROWS + HALO, cols],
                             preferred_element_type=F32)
            yg = jnp.dot(pooled.astype(BF16), pw_ref[g], preferred_element_type=F32)
            pool_ref[r0:r0 + POOL_ROWS, cols] = (yg * ps_ref[:, cols]).astype(BF16)

    nc = 256
    for r0 in range(0, tm, MIX_ROWS):
        rows = slice(r0, r0 + MIX_ROWS)
        for base in range(r0, r0 + MIX_ROWS, CONV_RB):
            conv_rows(base)
        for c in range(0, D_MODEL, nc):
            cols = slice(c, c + nc)
            y_conv = jnp.dot(cs_ref[rows, :], wco_ref[:, cols], preferred_element_type=F32)
            y_attn = jnp.dot(o_ref[rows, :], wao_ref[:, cols], preferred_element_type=F32)
            y_pool = jnp.dot(pool_ref[rows, :], wpo_ref[:, cols], preferred_element_type=F32)
            g0 = gate_ref[rows, c:c + nc].astype(F32)
            g1 = gate_ref[rows, D_MODEL + c:D_MODEL + c + nc].astype(F32)
            g2 = gate_ref[rows, 2 * D_MODEL + c:2 * D_MODEL + c + nc].astype(F32)
            merged_ref[rows, cols] = (g0 * y_conv + g1 * y_attn + g2 * y_pool).astype(BF16)
        mix = jnp.dot(merged_ref[rows, :], wout_ref[...], preferred_element_type=F32)
        out_ref[rows, :] = x_ref[rows, :] + _rmsnorm(mix, gpost_ref[...])


def _mixer(x2, u, pu, o, gates, dww, dwb, lng, lnb, wco, wao, pw, ps, wpo, wout, gpost, seq, layer):
    n = x2.shape[0]
    tm = TM_MIX
    tiles_per_seq = seq // tm
    row = lambda width: pl.BlockSpec((tm, width), lambda i: (i, 0))
    halo = lambda width: pl.BlockSpec(
        (HALO, width), lambda i: (jnp.maximum(i * (tm // HALO) - 1, 0), 0))
    return pl.pallas_call(
        functools.partial(_mixer_kernel, tiles_per_seq=tiles_per_seq),
        grid=(n // tm,),
        in_specs=[
            row(D_MODEL), row(CONV_CH), halo(CONV_CH), row(POOL_CH), halo(POOL_CH),
            row(ATTN_W), row(N_BRANCH * D_MODEL),
            pl.BlockSpec((2, len(POOL_WINDOWS), POOL_ROWS, POOL_ROWS + HALO), lambda i: (0, 0, 0, 0),
                         pipeline_mode=pl.Buffered(1)),
            _resident((CONV_WIDTH, CONV_CH), layer), _resident((1, CONV_CH), layer),
            _resident((1, CONV_CH), layer), _resident((1, CONV_CH), layer),
            _resident((CONV_CH, D_MODEL), layer), _resident((ATTN_W, D_MODEL), layer),
            _resident((len(POOL_WINDOWS), POOL_GC, POOL_GC), layer), _resident((1, POOL_CH), layer),
            _resident((POOL_CH, D_MODEL), layer), _resident((D_MODEL, D_MODEL), layer),
            _resident((1, D_MODEL), layer),
        ],
        out_specs=row(D_MODEL),
        out_shape=jax.ShapeDtypeStruct((n, D_MODEL), F32),
        scratch_shapes=[
            pltpu.VMEM((CONV_CH // LANES, tm + HALO, LANES), F32),
            pltpu.VMEM((tm + HALO, POOL_CH), BF16),
            pltpu.VMEM((tm, CONV_CH), BF16),
            pltpu.VMEM((tm, POOL_CH), BF16),
            pltpu.VMEM((tm, D_MODEL), BF16),
        ],
        compiler_params=pltpu.CompilerParams(
            dimension_semantics=("arbitrary",), vmem_limit_bytes=VMEM_LIMIT),
        name="mixer",
    )(x2, u, u, pu, pu, o, gates, _pool_bands(), dww, dwb, lng, lnb, wco, wao, pw, ps, wpo, wout, gpost)


def _mlp_kernel(x_ref, p_ref, gpre_ref, w1_ref, w2_ref, gpost_ref, wpp_ref, wpg_ref,
                out_ref, hid_ref):
    x = x_ref[...]
    h = _rmsnorm(x, gpre_ref[...]).astype(BF16)
    fc = 512
    for c in range(0, D_FF, fc):
        a = jnp.dot(h, w1_ref[:, c:c + fc], preferred_element_type=F32)
        a = jnp.maximum(a, 0.0)
        hid_ref[:, c:c + fc] = (a * a).astype(BF16)
    f = jnp.dot(hid_ref[...], w2_ref[...], preferred_element_type=F32)
    x = x + _rmsnorm(f, gpost_ref[...])
    gate = _sigmoid(jnp.dot(x.astype(BF16), wpg_ref[...], preferred_element_type=F32))
    pe = jnp.dot(p_ref[...].astype(BF16), wpp_ref[...], preferred_element_type=F32)
    out_ref[...] = x + gate * pe


def _mlp(x2, p3, gpre, w1, w2, gpost, wpp, wpg, layer):
    n = x2.shape[0]
    tm = TM_MLP
    row = lambda width: pl.BlockSpec((tm, width), lambda i: (i, 0))
    return pl.pallas_call(
        _mlp_kernel,
        grid=(n // tm,),
        in_specs=[
            row(D_MODEL), pl.BlockSpec((None, tm, PLE_DIM), lambda i: (layer, i, 0)),
            _resident((1, D_MODEL), layer),
            _resident((D_MODEL, D_FF), layer), _resident((D_FF, D_MODEL), layer),
            _resident((1, D_MODEL), layer),
            _resident((PLE_DIM, D_MODEL), layer), _resident((D_MODEL, D_MODEL), layer),
        ],
        out_specs=row(D_MODEL),
        out_shape=jax.ShapeDtypeStruct((n, D_MODEL), F32),
        scratch_shapes=[pltpu.VMEM((tm, D_FF), BF16)],
        compiler_params=pltpu.CompilerParams(
            dimension_semantics=("arbitrary",), vmem_limit_bytes=VMEM_LIMIT),
        name="mlp_ple",
    )(x2, p3, gpre, w1, w2, gpost, wpp, wpg)


def kernel(x, p, rel_bias, g_pre_mix, w_in, conv_dw_w, conv_dw_b, conv_ln_g, conv_ln_b, w_conv_out, lam_p, subln_g, w_attn_out, pool_w, pool_scale, w_pool_out, w_out, g_post_mix, g_pre_mlp, w_mlp_in, w_mlp_out, g_post_mlp, w_ple_proj, w_ple_gate):
    batch, seq, d = x.shape
    depth = w_in.shape[0]
    assert d == D_MODEL and seq % TM_MIX == 0 and seq % TQ == 0 and TQ == TK
    n = batch * seq
    x2 = x.reshape(n, d)
    bias_blocks = _near_bias_blocks(rel_bias)
    rows = lambda a: a.reshape(depth, 1, -1)
    bf = lambda a: a.astype(BF16)
    w_in_bf = bf(w_in)
    wvt_bf = bf(jnp.swapaxes(lax.optimization_barrier(w_in[:, :, V_COL:V_COL + ATTN_W]), 1, 2))
    p3 = p.reshape(depth, n, PLE_DIM)
    mixer_params = (conv_dw_w.reshape(depth, CONV_WIDTH, CONV_CH), rows(conv_dw_b),
                    rows(conv_ln_g), rows(conv_ln_b), bf(w_conv_out), bf(w_attn_out),
                    bf(pool_w), rows(pool_scale), bf(w_pool_out), bf(w_out), rows(g_post_mix))
    mlp_params = (rows(g_pre_mlp), bf(w_mlp_in), bf(w_mlp_out), rows(g_post_mlp),
                  bf(w_ple_proj), bf(w_ple_gate))
    g_pre, subln = rows(g_pre_mix), rows(subln_g)
    for layer in range(depth):
        u, q, k, vt, pu, gates = _proj_in(x2, g_pre, w_in_bf, wvt_bf, layer)
        o = _attention(q, k, vt, bias_blocks, lam_p, subln, batch, seq, layer)
        x2 = _mixer(x2, u, pu, o, gates, *mixer_params, seq, layer)
        x2 = _mlp(x2, p3, *mlp_params, layer)
    return x2.reshape(batch, seq, d)
```

```python
import functools
import math

import jax
import jax.numpy as jnp
import numpy as np
from jax import lax
from jax.experimental import pallas as pl
from jax.experimental.pallas import tpu as pltpu

F32 = jnp.float32
BF16 = jnp.bfloat16

D_MODEL = 1024
PLE_DIM = 256
CONV_CH = 512
CONV_WIDTH = 31
ATTN_HEADS = 4
ATTN_HD = 64
ATTN_VD = 2 * ATTN_HD
ATTN_W = ATTN_HEADS * ATTN_VD
POOL_CH = 512
POOL_WINDOWS = (2, 4, 8, 16)
POOL_GC = POOL_CH // len(POOL_WINDOWS)
N_BRANCH = 3
D_FF = 4 * D_MODEL
REL_BUCKETS = 32
REL_MAX_DIST = 128
EPS = 1e-6
MASK_VALUE = -1e30

TM_PROJ = 1024
PROJ_ROWS = 256
TM_MIX = 1024
TM_MLP = 1024
TQ = 2048
TK = 2048
SUB = 128
HEADS_PER_STEP = 1
ONES_ROWS = 16
ACC_ROWS = ATTN_VD + ONES_ROWS
LOG2E = math.log2(math.e)
HALO = 32
CONV_RB = 64
MIX_ROWS = 256
LANES = 128
VMEM_LIMIT = 60000 * 1024


def _sigmoid(x):
    return 0.5 * jnp.tanh(0.5 * x) + 0.5


def _rmsnorm(x, g):
    return x * lax.rsqrt(jnp.mean(x * x, axis=-1, keepdims=True) + EPS) * g


def _resident(shape, layer):
    index = (layer,) + (0,) * len(shape)
    return pl.BlockSpec((None,) + tuple(shape), lambda *_: index, pipeline_mode=pl.Buffered(1))


V_COL = 2 * CONV_CH + 2 * ATTN_W


def _proj_in_kernel(x_ref, g_ref, w_ref, wvt_ref, u_ref, q_ref, k_ref, vt_ref, pu_ref, gate_ref):
    for r0 in range(0, TM_PROJ, PROJ_ROWS):
        rows = slice(r0, r0 + PROJ_ROWS)
        h = _rmsnorm(x_ref[rows, :], g_ref[...]).astype(BF16)

        def proj(c0, width):
            return jnp.dot(h, w_ref[:, c0:c0 + width], preferred_element_type=F32)

        ca = proj(0, CONV_CH)
        cb = proj(CONV_CH, CONV_CH)
        u_ref[rows, :] = ca * _sigmoid(cb)
        off = 2 * CONV_CH
        q_ref[rows, :] = (proj(off, ATTN_W) * (ATTN_HD ** -0.5 * LOG2E)).astype(BF16)
        k_ref[rows, :] = proj(off + ATTN_W, ATTN_W).astype(BF16)
        vt_ref[:, rows] = lax.dot_general(wvt_ref[...], h, (((1,), (1,)), ((), ())),
                                          preferred_element_type=F32).astype(BF16)
        pu_ref[rows, :] = proj(V_COL + ATTN_W, POOL_CH)
        off = V_COL + ATTN_W + POOL_CH
        for c in range(0, N_BRANCH * D_MODEL, 512):
            gate_ref[rows, c:c + 512] = _sigmoid(proj(off + c, 512)).astype(BF16)


def _proj_in(x2, g, w_bf, wvt_bf, layer):
    n = x2.shape[0]
    d_in = w_bf.shape[-1]
    row = lambda width: pl.BlockSpec((TM_PROJ, width), lambda i: (i, 0))
    widths = (CONV_CH, ATTN_W, ATTN_W, None, POOL_CH, N_BRANCH * D_MODEL)
    out_specs = [row(w) if w else pl.BlockSpec((ATTN_W, TM_PROJ), lambda i: (0, i)) for w in widths]
    dtypes = (F32, BF16, BF16, BF16, F32, BF16)
    out_shape = [jax.ShapeDtypeStruct((n, w) if w else (ATTN_W, n), dt) for w, dt in zip(widths, dtypes)]
    return pl.pallas_call(
        _proj_in_kernel,
        grid=(n // TM_PROJ,),
        in_specs=[row(D_MODEL), _resident((1, D_MODEL), layer), _resident((D_MODEL, d_in), layer),
                  _resident((ATTN_W, D_MODEL), layer)],
        out_specs=out_specs,
        out_shape=out_shape,
        compiler_params=pltpu.CompilerParams(
            dimension_semantics=("arbitrary",), vmem_limit_bytes=VMEM_LIMIT),
        name="proj_in",
    )(x2, g, w_bf, wvt_bf)


def _rel_bucket_table(n_max):
    n = np.arange(n_max, dtype=np.int32)
    max_exact = REL_BUCKETS // 2
    nf = np.maximum(n, 1).astype(np.float32)
    large = max_exact + (np.log(nf / max_exact) / np.float32(math.log(REL_MAX_DIST / max_exact))
                         * (REL_BUCKETS - max_exact)).astype(np.int32)
    large = np.minimum(large, REL_BUCKETS - 1)
    return np.where(n < max_exact, n, large)


def _bias_block_nonzero(variant, delta):
    min_dist = variant * TK + delta * SUB - (SUB - 1)
    far_from = int(np.argmax(_rel_bucket_table(4 * TK) == REL_BUCKETS - 1))
    return min_dist < far_from


def _bias_block_index():
    nblk = TK // SUB
    needed = [(0, d) for d in range(nblk) if _bias_block_nonzero(0, d)]
    needed += [(1, d) for d in range(-(nblk - 1), nblk) if _bias_block_nonzero(1, d)]
    return {key: idx for idx, key in enumerate(needed)}


def _near_bias_blocks(rel_bias):
    assert (_rel_bucket_table(8 * TK)[TK + 1:] == REL_BUCKETS - 1).all()
    heads = rel_bias.shape[1]
    win = 2 * SUB - 1
    dist = np.stack([variant * TK + delta * SUB - (SUB - 1) + np.arange(win)
                     for variant, delta in _bias_block_index()])
    bucket = _rel_bucket_table(TK + TQ)[np.maximum(dist, 0)]
    shifted = (rel_bias.astype(F32) - rel_bias[REL_BUCKETS - 1].astype(F32)) * LOG2E
    w = jnp.where(dist[None] >= 0, jnp.transpose(shifted[bucket], (2, 0, 1)), MASK_VALUE)
    w = jnp.pad(w, ((0, 0), (0, 0), (0, 1)))
    m = jnp.tile(w, (1, 1, SUB))[:, :, :SUB * win].reshape(heads, dist.shape[0], SUB, win)
    return m[:, :, :, SUB - 1:2 * SUB - 1]


def _chain_rows(seq):
    return tuple((c + 1) * SUB if seq == TQ else TK for c in range(TQ // SUB))


def _attn_kernel(q_ref, k_ref, vt_ref, bias_ref, lamp_ref, g_ref, o_ref,
                 qb_ref, m_ref, acc_ref, s_ref, p_ref, *, lambda_init):
    lp = lamp_ref[...]
    lam = (jnp.exp(jnp.sum(lp[0:1] * lp[1:2], axis=-1, keepdims=True))
           - jnp.exp(jnp.sum(lp[2:3] * lp[3:4], axis=-1, keepdims=True)) + lambda_init)
    refs = (q_ref, k_ref, vt_ref, bias_ref, g_ref, o_ref, qb_ref, m_ref, acc_ref, s_ref, p_ref)
    n_tiles = q_ref.shape[0] // TQ
    if n_tiles == 1:
        _attn_q_tile(0, lam, *refs, lambda_init)
    else:
        def q_tile(i, carry):
            _attn_q_tile(i, lam, *refs, lambda_init)
            return carry

        lax.fori_loop(0, n_tiles, q_tile, 0)


def _attn_q_tile(i, lam, q_ref, k_ref, vt_ref, bias_ref, g_ref, o_ref,
                 qb_ref, m_ref, acc_ref, s_ref, p_ref, lambda_init):
    nsub = TQ // SUB
    single = isinstance(i, int)
    row0 = np.concatenate([[0], np.cumsum(_chain_rows(TQ if single else None))]).tolist()
    crow = lambda c, nk: slice(row0[c], row0[c] + nk)
    chains = [(ch, ch // nsub, ch % nsub) for ch in range(HEADS_PER_STEP * nsub)]
    hcols = lambda hh: slice(hh * ATTN_VD, (hh + 1) * ATTN_VD)
    bias_index = _bias_block_index()
    q0 = 0 if single else pl.multiple_of(i * TQ, TQ)
    lane = lax.broadcasted_iota(jnp.int32, (SUB, ATTN_VD), 1)
    for ch, hh, c in chains:
        qc = q_ref[pl.ds(q0 + c * SUB, SUB), hcols(hh)]
        zero = jnp.zeros_like(qc)
        qb_ref[ch, 0:SUB, :] = jnp.where(lane < ATTN_HD, qc, zero)
        qb_ref[ch, SUB:2 * SUB, :] = jnp.where(lane >= ATTN_HD, qc, zero)
        if not single:
            m_ref[ch] = jnp.full((1, 2 * SUB), MASK_VALUE, F32)
            acc_ref[ch] = jnp.zeros((ACC_ROWS, 2 * SUB), F32)

    def step(ks, n_keys, variant):
        tile_max = []
        for ch, hh, c in chains:
            nk = n_keys[c]
            s = lax.dot_general(k_ref[pl.ds(ks, nk), hcols(hh)], qb_ref[ch], (((1,), (1,)), ((), ())),
                                preferred_element_type=F32)
            if variant is not None:
                parts = []
                for bk in range(nk // SUB):
                    blk = s[bk * SUB:(bk + 1) * SUB, :]
                    if (variant, c - bk) in bias_index:
                        b = bias_ref[hh, bias_index[(variant, c - bk)]]
                        b = jnp.concatenate([b, b], axis=1)
                        blk = blk + b
                        if variant == 0 and c == bk:
                            blk = jnp.where(b == MASK_VALUE, MASK_VALUE, blk)
                    parts.append(blk)
                s = jnp.concatenate(parts, axis=0)
            s_ref[hh, crow(c, nk), :] = s
            tile_max.append(jnp.max(s, axis=0, keepdims=True))
        alphas = []
        for ch, hh, c in chains:
            nk = n_keys[c]
            if single:
                m_new = tile_max[ch]
            else:
                m_old = m_ref[ch]
                m_new = jnp.maximum(m_old, tile_max[ch])
                alphas.append(jnp.exp2(m_old - m_new))
                m_ref[ch] = m_new
            p_ref[hh, crow(c, nk), :] = jnp.exp2(s_ref[hh, crow(c, nk), :] - m_new).astype(BF16)
        for ch, hh, c in chains:
            nk = n_keys[c]
            vte = jnp.concatenate([vt_ref[hcols(hh), pl.ds(ks, nk)], jnp.ones((ONES_ROWS, nk), BF16)],
                                  axis=0)
            pv = jnp.dot(vte, p_ref[hh, crow(c, nk), :], preferred_element_type=F32)
            acc_ref[ch] = pv if single else alphas[ch] * acc_ref[ch] + pv

    full = (TK,) * nsub

    def far(j, carry):
        step(pl.multiple_of(j * TK, TK), full, None)
        return carry

    if not single:
        lax.fori_loop(0, jnp.maximum(i - 1, 0), far, 0)

        @pl.when(i >= 1)
        def _():
            step(pl.multiple_of((i - 1) * TK, TK), full, 1)

    step(0 if single else pl.multiple_of(i * TK, TK), tuple((c + 1) * SUB for c in range(nsub)), 0)

    for ch, hh, c in chains:
        acc = acc_ref[ch, 0:ATTN_VD, :]
        l = acc_ref[ch, ATTN_VD:ATTN_VD + 1, :]
        ot = acc[:, :SUB] / l[:, :SUB] - lam * (acc[:, SUB:] / l[:, SUB:])
        o = _rmsnorm(ot.T, g_ref[...]) * (1.0 - lambda_init)
        o_ref[pl.ds(q0 + c * SUB, SUB), hcols(hh)] = o.astype(BF16)


def _attention(q, k, vt, bias_blocks, lam_p, subln_g, batch, seq, layer):
    n = batch * seq
    nchain = HEADS_PER_STEP * (TQ // SUB)
    hw = HEADS_PER_STEP * ATTN_VD
    lambda_init = 0.8 - 0.6 * math.exp(-0.3 * layer)
    return pl.pallas_call(
        functools.partial(_attn_kernel, lambda_init=lambda_init),
        grid=(batch, ATTN_HEADS // HEADS_PER_STEP),
        in_specs=[
            pl.BlockSpec((seq, hw), lambda b, h: (b, h)),
            pl.BlockSpec((seq, hw), lambda b, h: (b, h)),
            pl.BlockSpec((hw, seq), lambda b, h: (h, b)),
            pl.BlockSpec((HEADS_PER_STEP, len(_bias_block_index()), SUB, SUB),
                         lambda b, h: (h, 0, 0, 0)),
            pl.BlockSpec((None, 4, ATTN_HD), lambda b, h: (layer, 0, 0)),
            pl.BlockSpec((None, 1, ATTN_VD), lambda b, h: (layer, 0, 0)),
        ],
        out_specs=pl.BlockSpec((seq, hw), lambda b, h: (b, h)),
        out_shape=jax.ShapeDtypeStruct((n, ATTN_W), BF16),
        scratch_shapes=[
            pltpu.VMEM((nchain, 2 * SUB, ATTN_VD), BF16),
            pltpu.VMEM((nchain, 1, 2 * SUB), F32),
            pltpu.VMEM((nchain, ACC_ROWS, 2 * SUB), F32),
            pltpu.VMEM((HEADS_PER_STEP, sum(_chain_rows(seq)), 2 * SUB), F32),
            pltpu.VMEM((HEADS_PER_STEP, sum(_chain_rows(seq)), 2 * SUB), BF16),
        ],
        compiler_params=pltpu.CompilerParams(
            dimension_semantics=("arbitrary", "arbitrary"),
            vmem_limit_bytes=VMEM_LIMIT),
        name="diff_attn",
    )(q, k, vt, bias_blocks, lam_p, subln_g)


def _mixer_kernel(x_ref, u_ref, uh_ref, pu_ref, ph_ref, o_ref, gate_ref,
                  dww_ref, dwb_ref, lng_ref, lnb_ref, wco_ref, wao_ref,
                  pw_ref, ps_ref, wpo_ref, wout_ref, gpost_ref,
                  out_ref,
                  ext_ref, pext_ref, cs_ref, pool_ref, merged_ref, *, tiles_per_seq):
    tm = TM_MIX
    t_in_seq = lax.rem(pl.program_id(0), tiles_per_seq)
    keep = t_in_seq > 0
    n_slab = CONV_CH // LANES

    def fill(dst_ref, tile_ref, halo_ref):
        for s in range(n_slab):
            cols = slice(s * LANES, (s + 1) * LANES)
            dst_ref[s, 0:HALO, :] = jnp.where(keep, halo_ref[:, cols].astype(F32), 0.0)
            dst_ref[s, HALO:HALO + tm, :] = tile_ref[:, cols].astype(F32)

    fill(ext_ref, u_ref, uh_ref)
    first_off = HALO - (CONV_WIDTH - 1)

    def conv_rows(base):
        conv = []
        for s in range(n_slab):
            cols = slice(s * LANES, (s + 1) * LANES)
            acc = jnp.zeros((CONV_RB, LANES), F32)
            for j in range(CONV_WIDTH):
                acc = acc + ext_ref[s, pl.ds(base + first_off + j, CONV_RB), :] * dww_ref[j:j + 1, cols]
            conv.append(acc + dwb_ref[:, cols])
        mu = sum(jnp.sum(a, axis=-1, keepdims=True) for a in conv) * (1.0 / CONV_CH)
        ex2 = sum(jnp.sum(a * a, axis=-1, keepdims=True) for a in conv) * (1.0 / CONV_CH)
        var = jnp.maximum(ex2 - mu * mu, 0.0)
        cen = [a - mu for a in conv]
        inv = lax.rsqrt(var + EPS)
        for s in range(n_slab):
            cols = slice(s * LANES, (s + 1) * LANES)
            y = cen[s] * inv * lng_ref[:, cols] + lnb_ref[:, cols]
            cs_ref[pl.ds(base, CONV_RB), cols] = (y * _sigmoid(y)).astype(BF16)

    fill(pext_ref, pu_ref, ph_ref)
    pos = t_in_seq * tm + lax.broadcasted_iota(jnp.int32, (tm, POOL_GC), 0)
    for g, w in enumerate(POOL_WINDOWS):
        cols = slice(g * POOL_GC, (g + 1) * POOL_GC)
        e = pext_ref[g, HALO:HALO + tm, :]
        tot = e
        for j in range(1, w):
            tot = tot + pext_ref[g, HALO - j:HALO - j + tm, :]
        cnt = jnp.minimum(pos + 1, w).astype(F32)
        pooled = tot / cnt - e
        yg = jnp.dot(pooled.astype(BF16), pw_ref[g], preferred_element_type=F32)
        pool_ref[:, cols] = (yg * ps_ref[:, cols]).astype(BF16)

    nc = 256
    for r0 in range(0, tm, MIX_ROWS):
        rows = slice(r0, r0 + MIX_ROWS)
        for base in range(r0, r0 + MIX_ROWS, CONV_RB):
            conv_rows(base)
        for c in range(0, D_MODEL, nc):
            cols = slice(c, c + nc)
            y_conv = jnp.dot(cs_ref[rows, :], wco_ref[:, cols], preferred_element_type=F32)
            y_attn = jnp.dot(o_ref[rows, :], wao_ref[:, cols], preferred_element_type=F32)
            y_pool = jnp.dot(pool_ref[rows, :], wpo_ref[:, cols], preferred_element_type=F32)
            g0 = gate_ref[rows, c:c + nc].astype(F32)
            g1 = gate_ref[rows, D_MODEL + c:D_MODEL + c + nc].astype(F32)
            g2 = gate_ref[rows, 2 * D_MODEL + c:2 * D_MODEL + c + nc].astype(F32)
            merged_ref[rows, cols] = (g0 * y_conv + g1 * y_attn + g2 * y_pool).astype(BF16)
        mix = jnp.dot(merged_ref[rows, :], wout_ref[...], preferred_element_type=F32)
        out_ref[rows, :] = x_ref[rows, :] + _rmsnorm(mix, gpost_ref[...])


def _mixer(x2, u, pu, o, gates, dww, dwb, lng, lnb, wco, wao, pw, ps, wpo, wout, gpost, seq, layer):
    n = x2.shape[0]
    tm = TM_MIX
    tiles_per_seq = seq // tm
    row = lambda width: pl.BlockSpec((tm, width), lambda i: (i, 0))
    halo = lambda width: pl.BlockSpec(
        (HALO, width), lambda i: (jnp.maximum(i * (tm // HALO) - 1, 0), 0))
    return pl.pallas_call(
        functools.partial(_mixer_kernel, tiles_per_seq=tiles_per_seq),
        grid=(n // tm,),
        in_specs=[
            row(D_MODEL), row(CONV_CH), halo(CONV_CH), row(POOL_CH), halo(POOL_CH),
            row(ATTN_W), row(N_BRANCH * D_MODEL),
            _resident((CONV_WIDTH, CONV_CH), layer), _resident((1, CONV_CH), layer),
            _resident((1, CONV_CH), layer), _resident((1, CONV_CH), layer),
            _resident((CONV_CH, D_MODEL), layer), _resident((ATTN_W, D_MODEL), layer),
            _resident((len(POOL_WINDOWS), POOL_GC, POOL_GC), layer), _resident((1, POOL_CH), layer),
            _resident((POOL_CH, D_MODEL), layer), _resident((D_MODEL, D_MODEL), layer),
            _resident((1, D_MODEL), layer),
        ],
        out_specs=row(D_MODEL),
        out_shape=jax.ShapeDtypeStruct((n, D_MODEL), F32),
        scratch_shapes=[
            pltpu.VMEM((CONV_CH // LANES, tm + HALO, LANES), F32),
            pltpu.VMEM((POOL_CH // LANES, tm + HALO, LANES), F32),
            pltpu.VMEM((tm, CONV_CH), BF16),
            pltpu.VMEM((tm, POOL_CH), BF16),
            pltpu.VMEM((tm, D_MODEL), BF16),
        ],
        compiler_params=pltpu.CompilerParams(
            dimension_semantics=("arbitrary",), vmem_limit_bytes=VMEM_LIMIT),
        name="mixer",
    )(x2, u, u, pu, pu, o, gates, dww, dwb, lng, lnb, wco, wao, pw, ps, wpo, wout, gpost)


def _mlp_kernel(x_ref, p_ref, gpre_ref, w1_ref, w2_ref, gpost_ref, wpp_ref, wpg_ref,
                out_ref, hid_ref):
    x = x_ref[...]
    h = _rmsnorm(x, gpre_ref[...]).astype(BF16)
    fc = 512
    for c in range(0, D_FF, fc):
        a = jnp.dot(h, w1_ref[:, c:c + fc], preferred_element_type=F32)
        a = jnp.maximum(a, 0.0)
        hid_ref[:, c:c + fc] = (a * a).astype(BF16)
    f = jnp.dot(hid_ref[...], w2_ref[...], preferred_element_type=F32)
    x = x + _rmsnorm(f, gpost_ref[...])
    gate = _sigmoid(jnp.dot(x.astype(BF16), wpg_ref[...], preferred_element_type=F32))
    pe = jnp.dot(p_ref[...].astype(BF16), wpp_ref[...], preferred_element_type=F32)
    out_ref[...] = x + gate * pe


def _mlp(x2, p3, gpre, w1, w2, gpost, wpp, wpg, layer):
    n = x2.shape[0]
    tm = TM_MLP
    row = lambda width: pl.BlockSpec((tm, width), lambda i: (i, 0))
    return pl.pallas_call(
        _mlp_kernel,
        grid=(n // tm,),
        in_specs=[
            row(D_MODEL), pl.BlockSpec((None, tm, PLE_DIM), lambda i: (layer, i, 0)),
            _resident((1, D_MODEL), layer),
            _resident((D_MODEL, D_FF), layer), _resident((D_FF, D_MODEL), layer),
            _resident((1, D_MODEL), layer),
            _resident((PLE_DIM, D_MODEL), layer), _resident((D_MODEL, D_MODEL), layer),
        ],
        out_specs=row(D_MODEL),
        out_shape=jax.ShapeDtypeStruct((n, D_MODEL), F32),
        scratch_shapes=[pltpu.VMEM((tm, D_FF), BF16)],
        compiler_params=pltpu.CompilerParams(
            dimension_semantics=("arbitrary",), vmem_limit_bytes=VMEM_LIMIT),
        name="mlp_ple",
    )(x2, p3, gpre, w1, w2, gpost, wpp, wpg)


def kernel(x, p, rel_bias, g_pre_mix, w_in, conv_dw_w, conv_dw_b, conv_ln_g, conv_ln_b, w_conv_out, lam_p, subln_g, w_attn_out, pool_w, pool_scale, w_pool_out, w_out, g_post_mix, g_pre_mlp, w_mlp_in, w_mlp_out, g_post_mlp, w_ple_proj, w_ple_gate):
    batch, seq, d = x.shape
    depth = w_in.shape[0]
    assert d == D_MODEL and seq % TM_MIX == 0 and seq % TQ == 0 and TQ == TK
    n = batch * seq
    x2 = x.reshape(n, d)
    bias_blocks = _near_bias_blocks(rel_bias)
    rows = lambda a: a.reshape(depth, 1, -1)
    bf = lambda a: a.astype(BF16)
    w_in_bf = bf(w_in)
    wvt_bf = bf(jnp.swapaxes(lax.optimization_barrier(w_in[:, :, V_COL:V_COL + ATTN_W]), 1, 2))
    p3 = p.reshape(depth, n, PLE_DIM)
    mixer_params = (conv_dw_w.reshape(depth, CONV_WIDTH, CONV_CH), rows(conv_dw_b),
                    rows(conv_ln_g), rows(conv_ln_b), bf(w_conv_out), bf(w_attn_out),
                    bf(pool_w), rows(pool_scale), bf(w_pool_out), bf(w_out), rows(g_post_mix))
    mlp_params = (rows(g_pre_mlp), bf(w_mlp_in), bf(w_mlp_out), rows(g_post_mlp),
                  bf(w_ple_proj), bf(w_ple_gate))
    g_pre, subln = rows(g_pre_mix), rows(subln_g)
    for layer in range(depth):
        u, q, k, vt, pu, gates = _proj_in(x2, g_pre, w_in_bf, wvt_bf, layer)
        o = _attention(q, k, vt, bias_blocks, lam_p, subln, batch, seq, layer)
        x2 = _mixer(x2, u, pu, o, gates, *mixer_params, seq, layer)
        x2 = _mlp(x2, p3, *mlp_params, layer)
    return x2.reshape(batch, seq, d)
```
